```python
import jax
import jax.numpy as jnp
from jax import lax
import numpy as np


D_MODEL = 4096
BATCH = 1
SEQ = 16384
DEPTH = 2

D_A = D_MODEL // 2
A_GROUPS = 8
A_CHUNK = 128
D_B = D_MODEL // 2
POOL_WINDOWS = (2, 4, 8, 16)
B_GROUPS = len(POOL_WINDOWS)
D_C = D_MODEL // 2
CONV_WIDTH = 3
N_BRANCH = 3
D_FF = 256 * ((8 * D_MODEL + 3 * 256 - 1) // (3 * 256))
N_IN = 2 * D_A + D_B + 3 * D_C + N_BRANCH * D_MODEL
EPS = 1e-6

kernel_name = 'hybrid_gated_sgu_pool_shortconv'


def rms_norm(x, g):
    xf = x.astype(jnp.float32)
    xf = xf * lax.rsqrt(jnp.mean(xf * xf, axis=-1, keepdims=True) + EPS)
    return xf.astype(x.dtype) * g


def layer_norm(x, g, b):
    xf = x.astype(jnp.float32)
    mu = jnp.mean(xf, axis=-1, keepdims=True)
    var = jnp.mean(jnp.square(xf - mu), axis=-1, keepdims=True)
    return ((xf - mu) * lax.rsqrt(var + EPS)).astype(x.dtype) * g + b


def spatial_gating_mixer(z, ln_g, ln_b, w_s, b_s):
    bsz, s, _ = z.shape
    u, v = jnp.split(z, 2, axis=-1)
    v = layer_norm(v, ln_g, ln_b)
    vc = v.reshape(bsz, s // A_CHUNK, A_CHUNK, A_GROUPS, D_A // A_GROUPS)
    causal = jnp.tril(jnp.ones((A_CHUNK, A_CHUNK), dtype=bool))
    w = jnp.where(causal[None], w_s, jnp.zeros_like(w_s))
    mixed = jnp.einsum('gts,bnsgc->bntgc', w, vc) + b_s.T[:, :, None]
    return u * mixed.reshape(bsz, s, D_A)


def multiscale_pool_mixer(xb, w_pool, scale):
    bsz, s, _ = xb.shape
    dg = D_B // B_GROUPS
    xg = xb.astype(jnp.float32).reshape(bsz, s, B_GROUPS, dg)
    csum = jnp.cumsum(xg, axis=1)
    pos = jnp.arange(1, s + 1, dtype=jnp.float32)
    groups = []
    for g, win in enumerate(POOL_WINDOWS):
        c = csum[:, :, g]
        c_prev = jnp.pad(c, ((0, 0), (win, 0), (0, 0)))[:, :s]
        count = jnp.minimum(pos, float(win))[None, :, None]
        groups.append((c - c_prev) / count - xg[:, :, g])
    pooled = jnp.stack(groups, axis=2).astype(xb.dtype)
    y = jnp.einsum('bsgc,gcd->bsgd', pooled, w_pool).reshape(bsz, s, D_B)
    return y * scale


def short_conv_mixer(b_gate, c_gate, h, conv_w):
    z = c_gate * h
    z = lax.conv_general_dilated(
        z, conv_w[:, None, :].astype(z.dtype),
        window_strides=(1,), padding=[(CONV_WIDTH - 1, 0)],
        dimension_numbers=('NWC', 'WIO', 'NWC'), feature_group_count=D_C)
    return b_gate * z


def setup_inputs(seed: int = 0) -> dict:
    key = jax.random.key(seed)
    k = jax.random.split(key, 20)
    L = DEPTH
    dg = D_B // B_GROUPS

    def nrm(kk, shape, scale):
        return jax.random.normal(kk, shape, jnp.float32) * scale

    return {
        'x': nrm(k[0], (BATCH, SEQ, D_MODEL), 1.0),
        'norm_mix_g': 1.0 + nrm(k[1], (L, D_MODEL), 0.02),
        'w_in': nrm(k[2], (L, D_MODEL, N_IN), D_MODEL ** -0.5),
        'ln_a_g': 1.0 + nrm(k[3], (L, D_A), 0.02),
        'ln_a_b': nrm(k[4], (L, D_A), 0.02),
        'w_spatial': nrm(k[5], (L, A_GROUPS, A_CHUNK, A_CHUNK), A_CHUNK ** -0.5),
        'b_spatial': 1.0 + nrm(k[6], (L, A_GROUPS, A_CHUNK), 0.02),
        'w_pool': nrm(k[7], (L, B_GROUPS, dg, dg), dg ** -0.5),
        'pool_scale': 1.0 + nrm(k[8], (L, D_B), 0.02),
        'conv_w': nrm(k[9], (L, CONV_WIDTH, D_C), CONV_WIDTH ** -0.5),
        'w_branch_a': nrm(k[10], (L, D_A, D_MODEL), D_A ** -0.5),
        'w_branch_b': nrm(k[11], (L, D_B, D_MODEL), D_B ** -0.5),
        'w_branch_c': nrm(k[12], (L, D_C, D_MODEL), D_C ** -0.5),
        'w_out': nrm(k[13], (L, D_MODEL, D_MODEL), D_MODEL ** -0.5),
        'norm_ffn_g': 1.0 + nrm(k[14], (L, D_MODEL), 0.02),
        'w_ffn_gate': nrm(k[15], (L, D_MODEL, D_FF), D_MODEL ** -0.5),
        'w_ffn_up': nrm(k[16], (L, D_MODEL, D_FF), D_MODEL ** -0.5),
        'w_ffn_down': nrm(k[17], (L, D_FF, D_MODEL), D_FF ** -0.5),
        'final_norm_g': 1.0 + nrm(k[18], (D_MODEL,), 0.02),
    }


def reference(x, norm_mix_g, w_in, ln_a_g, ln_a_b, w_spatial, b_spatial, w_pool,
              pool_scale, conv_w, w_branch_a, w_branch_b, w_branch_c, w_out,
              norm_ffn_g, w_ffn_gate, w_ffn_up, w_ffn_down, final_norm_g):
    bsz, s, _ = x.shape
    cuts = [2 * D_A,
            2 * D_A + D_B,
            2 * D_A + D_B + D_C,
            2 * D_A + D_B + 2 * D_C,
            2 * D_A + D_B + 3 * D_C]
    for l in range(DEPTH):
        h = rms_norm(x, norm_mix_g[l])
        proj = jnp.einsum('bsd,dn->bsn', h, w_in[l])
        z_a, x_b, b_c, c_c, h_c, gate_logits = jnp.split(proj, cuts, axis=-1)
        y_a = spatial_gating_mixer(jax.nn.gelu(z_a), ln_a_g[l], ln_a_b[l],
                                   w_spatial[l], b_spatial[l])
        y_b = multiscale_pool_mixer(x_b, w_pool[l], pool_scale[l])
        y_c = short_conv_mixer(b_c, c_c, h_c, conv_w[l])
        gates = jax.nn.sigmoid(gate_logits.astype(jnp.float32)).astype(x.dtype)
        gates = gates.reshape(bsz, s, N_BRANCH, D_MODEL)
        merged = (gates[:, :, 0] * (y_a @ w_branch_a[l])
                  + gates[:, :, 1] * (y_b @ w_branch_b[l])
                  + gates[:, :, 2] * (y_c @ w_branch_c[l]))
        x = x + merged @ w_out[l]
        h = rms_norm(x, norm_ffn_g[l])
        ff = jax.nn.silu(h @ w_ffn_gate[l]) * (h @ w_ffn_up[l])
        x = x + ff @ w_ffn_down[l]
    return rms_norm(x, final_norm_g)
```

```python
import functools

import jax
import jax.numpy as jnp
from jax import lax
from jax.experimental import pallas as pl
from jax.experimental.pallas import tpu as pltpu

D_MODEL = 4096
D_HALF = D_MODEL // 2
A_GROUPS = 8
A_CHUNK = 128
POOL_WINDOWS = (2, 4, 8, 16)
POOL_HALO = 16
CONV_HALO = 8
N_IN = 12 * D_HALF
GATE_COL0 = 6 * D_HALF
EPS = 1e-6

V7X_VMEM_LIMIT_BYTES = 56 * 1024 * 1024

F32 = jnp.float32
BF16 = jnp.bfloat16


def _params(*semantics):
    return pltpu.CompilerParams(dimension_semantics=semantics,
                                vmem_limit_bytes=V7X_VMEM_LIMIT_BYTES)


def _rms_norm_bf16(x, g):
    ms = jnp.mean(x * x, axis=-1, keepdims=True)
    return ((x * lax.rsqrt(ms + EPS)) * g).astype(BF16)


def _inproj_kernel(x_ref, g_ref, w_ref, o_ref, h_ref, *, gelu_tiles, gate_tile0):
    j = pl.program_id(1)

    @pl.when(j == 0)
    def _():
        h_ref[...] = _rms_norm_bf16(x_ref[...], g_ref[...])

    def proj():
        return jnp.dot(h_ref[...], w_ref[...], preferred_element_type=F32)

    @pl.when(j < gelu_tiles)
    def _():
        o_ref[...] = jax.nn.gelu(proj())

    @pl.when((j >= gelu_tiles) & (j < gate_tile0))
    def _():
        o_ref[...] = proj()

    @pl.when(j >= gate_tile0)
    def _():
        o_ref[...] = jax.nn.sigmoid(proj())


def _inproj(x, g, w, *, tm=512, tn=1024):
    s, d = x.shape
    n = w.shape[1]
    kern = functools.partial(_inproj_kernel, gelu_tiles=(2 * D_HALF) // tn,
                             gate_tile0=GATE_COL0 // tn)
    return pl.pallas_call(
        kern,
        grid=(s // tm, n // tn),
        in_specs=[pl.BlockSpec((tm, d), lambda i, j: (i, 0)),
                  pl.BlockSpec((1, d), lambda i, j: (0, 0)),
                  pl.BlockSpec((d, tn), lambda i, j: (0, j))],
        out_specs=pl.BlockSpec((tm, tn), lambda i, j: (i, j)),
        out_shape=jax.ShapeDtypeStruct((s, n), F32),
        scratch_shapes=[pltpu.VMEM((tm, d), BF16)],
        compiler_params=_params("parallel", "arbitrary"),
        name="inproj",
    )(x, g, w)


def _branch_a_kernel(u_ref, v_ref, lng_ref, lnb_ref, ws_ref, bias_ref, y_ref):
    tm = u_ref.shape[0]
    gw = D_HALF // A_GROUPS
    v = v_ref[...]
    mu = jnp.mean(v, axis=-1, keepdims=True)
    dv = v - mu
    var = jnp.mean(dv * dv, axis=-1, keepdims=True)
    vn = ((dv * lax.rsqrt(var + EPS)) * lng_ref[...] + lnb_ref[...]).astype(BF16)
    t_idx = lax.broadcasted_iota(jnp.int32, (A_CHUNK, A_CHUNK), 0)
    s_idx = lax.broadcasted_iota(jnp.int32, (A_CHUNK, A_CHUNK), 1)
    causal = s_idx <= t_idx
    for g in range(A_GROUPS):
        wg = jnp.where(causal, ws_ref[g], 0.0).astype(BF16)
        cols = slice(g * gw, (g + 1) * gw)
        for n in range(tm // A_CHUNK):
            rows = slice(n * A_CHUNK, (n + 1) * A_CHUNK)
            mixed = jnp.dot(wg, vn[rows, cols], preferred_element_type=F32)
            mixed = mixed + bias_ref[:, cols]
            y_ref[rows, cols] = (u_ref[rows, cols] * mixed).astype(BF16)


def _branch_a(proj, ln_g, ln_b, w_s, bias, *, tm=256):
    s = proj.shape[0]
    return pl.pallas_call(
        _branch_a_kernel,
        grid=(s // tm,),
        in_specs=[pl.BlockSpec((tm, D_HALF), lambda i: (i, 0)),
                  pl.BlockSpec((tm, D_HALF), lambda i: (i, 1)),
                  pl.BlockSpec((1, D_HALF), lambda i: (0, 0)),
                  pl.BlockSpec((1, D_HALF), lambda i: (0, 0)),
                  pl.BlockSpec((A_GROUPS, A_CHUNK, A_CHUNK), lambda i: (0, 0, 0)),
                  pl.BlockSpec((A_CHUNK, D_HALF), lambda i: (0, 0))],
        out_specs=pl.BlockSpec((tm, D_HALF), lambda i: (i, 0)),
        out_shape=jax.ShapeDtypeStruct((s, D_HALF), BF16),
        compiler_params=_params("parallel"),
        name="branch_a",
    )(proj, proj, ln_g, ln_b, w_s, bias)


def _branch_b_kernel(x_ref, halo_ref, wp_ref, scale_ref, y_ref):
    i = pl.program_id(0)
    tm = x_ref.shape[0]
    gw = D_HALF // len(POOL_WINDOWS)
    pos = i * tm + lax.broadcasted_iota(jnp.int32, (tm, 1), 0) + 1
    for g, win in enumerate(POOL_WINDOWS):
        cols = slice(g * gw, (g + 1) * gw)
        x = x_ref[:, cols]
        ext = jnp.concatenate([jnp.where(i > 0, halo_ref[:, cols], 0.0), x], axis=0)
        span = 1
        acc = ext
        while span < win:
            acc = acc[span:, :] + acc[:-span, :]
            span *= 2
        wsum = acc[POOL_HALO - (win - 1):, :]
        count = jnp.minimum(pos, win).astype(F32)
        pooled = (wsum / count - x).astype(BF16)
        y = jnp.dot(pooled, wp_ref[g], preferred_element_type=F32)
        y_ref[:, cols] = (y * scale_ref[:, cols]).astype(BF16)


def _branch_b(proj, w_pool, scale, *, tm=256):
    s = proj.shape[0]
    gw = D_HALF // len(POOL_WINDOWS)
    hb = tm // POOL_HALO
    return pl.pallas_call(
        _branch_b_kernel,
        grid=(s // tm,),
        in_specs=[pl.BlockSpec((tm, D_HALF), lambda i: (i, 2)),
                  pl.BlockSpec((POOL_HALO, D_HALF),
                               lambda i: (jnp.maximum(i * hb - 1, 0), 2)),
                  pl.BlockSpec((len(POOL_WINDOWS), gw, gw), lambda i: (0, 0, 0)),
                  pl.BlockSpec((1, D_HALF), lambda i: (0, 0))],
        out_specs=pl.BlockSpec((tm, D_HALF), lambda i: (i, 0)),
        out_shape=jax.ShapeDtypeStruct((s, D_HALF), BF16),
        compiler_params=_params("parallel"),
        name="branch_b",
    )(proj, proj, w_pool, scale)


def _branch_c_kernel(b_ref, c_ref, h_ref, ch_ref, hh_ref, cw_ref, y_ref):
    i = pl.program_id(0)
    tm = b_ref.shape[0]
    z = c_ref[...] * h_ref[...]
    zh = jnp.where(i > 0, ch_ref[...] * hh_ref[...], 0.0)
    ext = jnp.concatenate([zh, z], axis=0)
    conv = (cw_ref[0:1, :] * ext[CONV_HALO - 2:CONV_HALO - 2 + tm, :]
            + cw_ref[1:2, :] * ext[CONV_HALO - 1:CONV_HALO - 1 + tm, :]
            + cw_ref[2:3, :] * z)
    y_ref[...] = (b_ref[...] * conv).astype(BF16)


def _branch_c(proj, conv_w, *, tm=256):
    s = proj.shape[0]
    hb = tm // CONV_HALO

    def halo_spec(col):
        return pl.BlockSpec((CONV_HALO, D_HALF),
                            lambda i: (jnp.maximum(i * hb - 1, 0), col))

    return pl.pallas_call(
        _branch_c_kernel,
        grid=(s // tm,),
        in_specs=[pl.BlockSpec((tm, D_HALF), lambda i: (i, 3)),
                  pl.BlockSpec((tm, D_HALF), lambda i: (i, 4)),
                  pl.BlockSpec((tm, D_HALF), lambda i: (i, 5)),
                  halo_spec(4), halo_spec(5),
                  pl.BlockSpec((3, D_HALF), lambda i: (0, 0))],
        out_specs=pl.BlockSpec((tm, D_HALF), lambda i: (i, 0)),
        out_shape=jax.ShapeDtypeStruct((s, D_HALF), BF16),
        compiler_params=_params("parallel"),
        name="branch_c",
    )(proj, proj, proj, proj, proj, conv_w)


def _merge_kernel(ya_ref, yb_ref, yc_ref, wa_ref, wb_ref, wc_ref,
                  ga_ref, gb_ref, gc_ref, o_ref):
    m = ga_ref[...] * jnp.dot(ya_ref[...], wa_ref[...], preferred_element_type=F32)
    m = m + gb_ref[...] * jnp.dot(yb_ref[...], wb_ref[...], preferred_element_type=F32)
    m = m + gc_ref[...] * jnp.dot(yc_ref[...], wc_ref[...], preferred_element_type=F32)
    o_ref[...] = m.astype(BF16)


def _merge(ya, yb, yc, wa, wb, wc, proj, *, tm=512, tn=512):
    s = ya.shape[0]
    g0 = GATE_COL0 // tn
    gstep = D_MODEL // tn
    y_spec = pl.BlockSpec((tm, D_HALF), lambda i, j: (i, 0))
    w_spec = pl.BlockSpec((D_HALF, tn), lambda i, j: (0, j))

    def gate_spec(k):
        return pl.BlockSpec((tm, tn), lambda i, j: (i, g0 + k * gstep + j))

    return pl.pallas_call(
        _merge_kernel,
        grid=(s // tm, D_MODEL // tn),
        in_specs=[y_spec, y_spec, y_spec, w_spec, w_spec, w_spec,
                  gate_spec(0), gate_spec(1), gate_spec(2)],
        out_specs=pl.BlockSpec((tm, tn), lambda i, j: (i, j)),
        out_shape=jax.ShapeDtypeStruct((s, D_MODEL), BF16),
        compiler_params=_params("parallel", "arbitrary"),
        name="merge",
    )(ya, yb, yc, wa, wb, wc, proj, proj, proj)


def _matmul_residual_kernel(a_ref, w_ref, x_ref, o_ref):
    o_ref[...] = x_ref[...] + jnp.dot(a_ref[...], w_ref[...], preferred_element_type=F32)


def _matmul_residual(a, w, x, *, tm, tn, name):
    s, k = a.shape
    n = w.shape[1]
    return pl.pallas_call(
        _matmul_residual_kernel,
        grid=(s // tm, n // tn),
        in_specs=[pl.BlockSpec((tm, k), lambda i, j: (i, 0)),
                  pl.BlockSpec((k, tn), lambda i, j: (0, j)),
                  pl.BlockSpec((tm, tn), lambda i, j: (i, j))],
        out_specs=pl.BlockSpec((tm, tn), lambda i, j: (i, j)),
        out_shape=jax.ShapeDtypeStruct((s, n), F32),
        compiler_params=_params("parallel", "arbitrary"),
        name=name,
    )(a, w, x)


def _ffn_up_kernel(x_ref, g_ref, wg_ref, wu_ref, o_ref, h_ref):
    @pl.when(pl.program_id(1) == 0)
    def _():
        h_ref[...] = _rms_norm_bf16(x_ref[...], g_ref[...])

    h = h_ref[...]
    gate = jnp.dot(h, wg_ref[...], preferred_element_type=F32)
    up = jnp.dot(h, wu_ref[...], preferred_element_type=F32)
    o_ref[...] = (jax.nn.silu(gate) * up).astype(BF16)


def _ffn_up(x, g, wg, wu, *, tm=512, tn=256):
    s, d = x.shape
    f = wg.shape[1]
    w_spec = pl.BlockSpec((d, tn), lambda i, j: (0, j))
    return pl.pallas_call(
        _ffn_up_kernel,
        grid=(s // tm, f // tn),
        in_specs=[pl.BlockSpec((tm, d), lambda i, j: (i, 0)),
                  pl.BlockSpec((1, d), lambda i, j: (0, 0)),
                  w_spec, w_spec],
        out_specs=pl.BlockSpec((tm, tn), lambda i, j: (i, j)),
        out_shape=jax.ShapeDtypeStruct((s, f), BF16),
        scratch_shapes=[pltpu.VMEM((tm, d), BF16)],
        compiler_params=_params("parallel", "arbitrary"),
        name="ffn_up",
    )(x, g, wg, wu)


def _final_norm_kernel(x_ref, g_ref, o_ref):
    x = x_ref[...]
    ms = jnp.mean(x * x, axis=-1, keepdims=True)
    o_ref[...] = (x * lax.rsqrt(ms + EPS)) * g_ref[...]


def _final_norm(x, g, *, tm=512):
    s, d = x.shape
    return pl.pallas_call(
        _final_norm_kernel,
        grid=(s // tm,),
        in_specs=[pl.BlockSpec((tm, d), lambda i: (i, 0)),
                  pl.BlockSpec((1, d), lambda i: (0, 0))],
        out_specs=pl.BlockSpec((tm, d), lambda i: (i, 0)),
        out_shape=jax.ShapeDtypeStruct((s, d), F32),
        compiler_params=_params("parallel"),
        name="final_norm",
    )(x, g)


def kernel(x, norm_mix_g, w_in, ln_a_g, ln_a_b, w_spatial, b_spatial, w_pool,
           pool_scale, conv_w, w_branch_a, w_branch_b, w_branch_c, w_out,
           norm_ffn_g, w_ffn_gate, w_ffn_up, w_ffn_down, final_norm_g):
    bsz, s, d = x.shape
    depth = w_in.shape[0]
    assert (bsz, d) == (1, D_MODEL) and w_in.shape[2] == N_IN
    xs = x.reshape(s, d)
    gw_a = D_HALF // A_GROUPS
    for l in range(depth):
        proj = _inproj(xs, norm_mix_g[l][None], w_in[l].astype(BF16))
        bias = jnp.repeat(b_spatial[l].T, gw_a, axis=1)
        ya = _branch_a(proj, ln_a_g[l][None], ln_a_b[l][None], w_spatial[l], bias)
        yb = _branch_b(proj, w_pool[l].astype(BF16), pool_scale[l][None])
        yc = _branch_c(proj, conv_w[l])
        merged = _merge(ya, yb, yc, w_branch_a[l].astype(BF16),
                        w_branch_b[l].astype(BF16), w_branch_c[l].astype(BF16), proj)
        xs = _matmul_residual(merged, w_out[l].astype(BF16), xs,
                              tm=1024, tn=1024, name="outproj")
        ff = _ffn_up(xs, norm_ffn_g[l][None], w_ffn_gate[l].astype(BF16),
                     w_ffn_up[l].astype(BF16))
        xs = _matmul_residual(ff, w_ffn_down[l].astype(BF16), xs,
                              tm=512, tn=512, name="ffn_down")
    return _final_norm(xs, final_norm_g[None]).reshape(bsz, s, d)
```

```python
import functools

import jax
import jax.numpy as jnp
from jax import lax
from jax.experimental import pallas as pl
from jax.experimental.pallas import tpu as pltpu

D_MODEL = 4096
D_HALF = D_MODEL // 2
A_GROUPS = 8
A_CHUNK = 128
POOL_WINDOWS = (2, 4, 8, 16)
POOL_HALO = 16
CONV_HALO = 8
N_IN = 12 * D_HALF
GATE_COL0 = 6 * D_HALF
EPS = 1e-6

V7X_VMEM_LIMIT_BYTES = 56 * 1024 * 1024

F32 = jnp.float32
BF16 = jnp.bfloat16


def _params(*semantics):
    return pltpu.CompilerParams(dimension_semantics=semantics,
                                vmem_limit_bytes=V7X_VMEM_LIMIT_BYTES)


def _rms_norm(x, g):
    ms = jnp.mean(x * x, axis=-1, keepdims=True)
    return (x * lax.rsqrt(ms + EPS)) * g


def _layer_vec_spec(width, layer, grid_rank):
    if grid_rank == 1:
        return pl.BlockSpec((None, 1, width), lambda i: (layer, 0, 0))
    return pl.BlockSpec((None, 1, width), lambda i, j: (layer, 0, 0))


def _rms_norm_kernel(x_ref, g_ref, o_ref):
    o_ref[...] = _rms_norm(x_ref[...], g_ref[...]).astype(o_ref.dtype)


def _rms_norm_call(x, g, layer, out_dtype, *, tm=256):
    s, d = x.shape
    return pl.pallas_call(
        _rms_norm_kernel,
        grid=(s // tm,),
        in_specs=[pl.BlockSpec((tm, d), lambda i: (i, 0)),
                  _layer_vec_spec(d, layer, 1)],
        out_specs=pl.BlockSpec((tm, d), lambda i: (i, 0)),
        out_shape=jax.ShapeDtypeStruct((s, d), out_dtype),
        compiler_params=_params("parallel"),
        name="rms_norm",
    )(x, g)


def _inproj_kernel(h_ref, w_ref, o_ref, *, gelu_tiles, gate_tile0):
    j = pl.program_id(1)

    def proj():
        return jnp.dot(h_ref[...], w_ref[...], preferred_element_type=F32)

    @pl.when(j < gelu_tiles)
    def _():
        o_ref[...] = jax.nn.gelu(proj())

    @pl.when((j >= gelu_tiles) & (j < gate_tile0))
    def _():
        o_ref[...] = proj()

    @pl.when(j >= gate_tile0)
    def _():
        o_ref[...] = jax.nn.sigmoid(proj())


def _inproj(h, w, layer, *, tm=1024, tn=1024):
    s, d = h.shape
    n = w.shape[2]
    kern = functools.partial(_inproj_kernel, gelu_tiles=(2 * D_HALF) // tn,
                             gate_tile0=GATE_COL0 // tn)
    return pl.pallas_call(
        kern,
        grid=(s // tm, n // tn),
        in_specs=[pl.BlockSpec((tm, d), lambda i, j: (i, 0)),
                  pl.BlockSpec((None, d, tn), lambda i, j: (layer, 0, j))],
        out_specs=pl.BlockSpec((tm, tn), lambda i, j: (i, j)),
        out_shape=jax.ShapeDtypeStruct((s, n), F32),
        compiler_params=_params("parallel", "arbitrary"),
        name="inproj",
    )(h, w)


def _branch_a_kernel(u_ref, v_ref, lng_ref, lnb_ref, ws_ref, bias_ref, y_ref):
    tm = u_ref.shape[0]
    gw = D_HALF // A_GROUPS
    v = v_ref[...]
    mu = jnp.mean(v, axis=-1, keepdims=True)
    dv = v - mu
    var = jnp.mean(dv * dv, axis=-1, keepdims=True)
    vn = ((dv * lax.rsqrt(var + EPS)) * lng_ref[...] + lnb_ref[...]).astype(BF16)
    t_idx = lax.broadcasted_iota(jnp.int32, (A_CHUNK, A_CHUNK), 0)
    s_idx = lax.broadcasted_iota(jnp.int32, (A_CHUNK, A_CHUNK), 1)
    causal = s_idx <= t_idx
    for g in range(A_GROUPS):
        wg = jnp.where(causal, ws_ref[g], 0.0).astype(BF16)
        cols = slice(g * gw, (g + 1) * gw)
        for n in range(tm // A_CHUNK):
            rows = slice(n * A_CHUNK, (n + 1) * A_CHUNK)
            mixed = jnp.dot(wg, vn[rows, cols], preferred_element_type=F32)
            mixed = mixed + bias_ref[:, cols]
            y_ref[rows, cols] = (u_ref[rows, cols] * mixed).astype(BF16)


def _branch_a(proj, ln_g, ln_b, w_s, bias, layer, *, tm=256):
    s = proj.shape[0]
    return pl.pallas_call(
        _branch_a_kernel,
        grid=(s // tm,),
        in_specs=[pl.BlockSpec((tm, D_HALF), lambda i: (i, 0)),
                  pl.BlockSpec((tm, D_HALF), lambda i: (i, 1)),
                  _layer_vec_spec(D_HALF, layer, 1),
                  _layer_vec_spec(D_HALF, layer, 1),
                  pl.BlockSpec((None, A_GROUPS, A_CHUNK, A_CHUNK),
                               lambda i: (layer, 0, 0, 0)),
                  pl.BlockSpec((A_CHUNK, D_HALF), lambda i: (0, 0))],
        out_specs=pl.BlockSpec((tm, D_HALF), lambda i: (i, 0)),
        out_shape=jax.ShapeDtypeStruct((s, D_HALF), BF16),
        compiler_params=_params("parallel"),
        name="branch_a",
    )(proj, proj, ln_g, ln_b, w_s, bias)


def _branch_b_kernel(x_ref, halo_ref, wp_ref, scale_ref, y_ref):
    i = pl.program_id(0)
    tm = x_ref.shape[0]
    gw = D_HALF // len(POOL_WINDOWS)
    pos = i * tm + lax.broadcasted_iota(jnp.int32, (tm, 1), 0) + 1
    for g, win in enumerate(POOL_WINDOWS):
        cols = slice(g * gw, (g + 1) * gw)
        x = x_ref[:, cols]
        ext = jnp.concatenate([jnp.where(i > 0, halo_ref[:, cols], 0.0), x], axis=0)
        span = 1
        acc = ext
        while span < win:
            acc = acc[span:, :] + acc[:-span, :]
            span *= 2
        wsum = acc[POOL_HALO - (win - 1):, :]
        count = jnp.minimum(pos, win).astype(F32)
        pooled = (wsum / count - x).astype(BF16)
        y = jnp.dot(pooled, wp_ref[g], preferred_element_type=F32)
        y_ref[:, cols] = (y * scale_ref[:, cols]).astype(BF16)


def _branch_b(proj, w_pool, scale, layer, *, tm=256):
    s = proj.shape[0]
    gw = D_HALF // len(POOL_WINDOWS)
    hb = tm // POOL_HALO
    return pl.pallas_call(
        _branch_b_kernel,
        grid=(s // tm,),
        in_specs=[pl.BlockSpec((tm, D_HALF), lambda i: (i, 2)),
                  pl.BlockSpec((POOL_HALO, D_HALF),
                               lambda i: (jnp.maximum(i * hb - 1, 0), 2)),
                  pl.BlockSpec((None, len(POOL_WINDOWS), gw, gw),
                               lambda i: (layer, 0, 0, 0)),
                  _layer_vec_spec(D_HALF, layer, 1)],
        out_specs=pl.BlockSpec((tm, D_HALF), lambda i: (i, 0)),
        out_shape=jax.ShapeDtypeStruct((s, D_HALF), BF16),
        compiler_params=_params("parallel"),
        name="branch_b",
    )(proj, proj, w_pool, scale)


def _branch_c_kernel(b_ref, c_ref, h_ref, ch_ref, hh_ref, cw_ref, y_ref):
    i = pl.program_id(0)
    tm = b_ref.shape[0]
    z = c_ref[...] * h_ref[...]
    zh = jnp.where(i > 0, ch_ref[...] * hh_ref[...], 0.0)
    ext = jnp.concatenate([zh, z], axis=0)
    conv = (cw_ref[0:1, :] * ext[CONV_HALO - 2:CONV_HALO - 2 + tm, :]
            + cw_ref[1:2, :] * ext[CONV_HALO - 1:CONV_HALO - 1 + tm, :]
            + cw_ref[2:3, :] * z)
    y_ref[...] = (b_ref[...] * conv).astype(BF16)


def _branch_c(proj, conv_w, layer, *, tm=256):
    s = proj.shape[0]
    hb = tm // CONV_HALO

    def halo_spec(col):
        return pl.BlockSpec((CONV_HALO, D_HALF),
                            lambda i: (jnp.maximum(i * hb - 1, 0), col))

    return pl.pallas_call(
        _branch_c_kernel,
        grid=(s // tm,),
        in_specs=[pl.BlockSpec((tm, D_HALF), lambda i: (i, 3)),
                  pl.BlockSpec((tm, D_HALF), lambda i: (i, 4)),
                  pl.BlockSpec((tm, D_HALF), lambda i: (i, 5)),
                  halo_spec(4), halo_spec(5),
                  pl.BlockSpec((None, 3, D_HALF), lambda i: (layer, 0, 0))],
        out_specs=pl.BlockSpec((tm, D_HALF), lambda i: (i, 0)),
        out_shape=jax.ShapeDtypeStruct((s, D_HALF), BF16),
        compiler_params=_params("parallel"),
        name="branch_c",
    )(proj, proj, proj, proj, proj, conv_w)


def _merge_kernel(ya_ref, yb_ref, yc_ref, wa_ref, wb_ref, wc_ref,
                  ga_ref, gb_ref, gc_ref, o_ref):
    m = ga_ref[...] * jnp.dot(ya_ref[...], wa_ref[...], preferred_element_type=F32)
    m = m + gb_ref[...] * jnp.dot(yb_ref[...], wb_ref[...], preferred_element_type=F32)
    m = m + gc_ref[...] * jnp.dot(yc_ref[...], wc_ref[...], preferred_element_type=F32)
    o_ref[...] = m.astype(BF16)


def _merge(ya, yb, yc, wa, wb, wc, proj, layer, *, tm=1024, tn=512):
    s = ya.shape[0]
    g0 = GATE_COL0 // tn
    gstep = D_MODEL // tn
    y_spec = pl.BlockSpec((tm, D_HALF), lambda i, j: (i, 0))
    w_spec = pl.BlockSpec((None, D_HALF, tn), lambda i, j: (layer, 0, j))

    def gate_spec(k):
        return pl.BlockSpec((tm, tn), lambda i, j: (i, g0 + k * gstep + j))

    return pl.pallas_call(
        _merge_kernel,
        grid=(s // tm, D_MODEL // tn),
        in_specs=[y_spec, y_spec, y_spec, w_spec, w_spec, w_spec,
                  gate_spec(0), gate_spec(1), gate_spec(2)],
        out_specs=pl.BlockSpec((tm, tn), lambda i, j: (i, j)),
        out_shape=jax.ShapeDtypeStruct((s, D_MODEL), BF16),
        compiler_params=_params("parallel", "arbitrary"),
        name="merge",
    )(ya, yb, yc, wa, wb, wc, proj, proj, proj)


def _matmul_residual_kernel(a_ref, w_ref, x_ref, o_ref):
    o_ref[...] = x_ref[...] + jnp.dot(a_ref[...], w_ref[...], preferred_element_type=F32)


def _matmul_residual(a, w, x, layer, *, tm, tn, name):
    s, k = a.shape
    n = w.shape[2]
    return pl.pallas_call(
        _matmul_residual_kernel,
        grid=(s // tm, n // tn),
        in_specs=[pl.BlockSpec((tm, k), lambda i, j: (i, 0)),
                  pl.BlockSpec((None, k, tn), lambda i, j: (layer, 0, j)),
                  pl.BlockSpec((tm, tn), lambda i, j: (i, j))],
        out_specs=pl.BlockSpec((tm, tn), lambda i, j: (i, j)),
        out_shape=jax.ShapeDtypeStruct((s, n), F32),
        compiler_params=_params("parallel", "arbitrary"),
        name=name,
    )(a, w, x)


def _ffn_up_kernel(h_ref, wg_ref, wu_ref, o_ref):
    h = h_ref[...]
    gate = jnp.dot(h, wg_ref[...], preferred_element_type=F32)
    up = jnp.dot(h, wu_ref[...], preferred_element_type=F32)
    o_ref[...] = (jax.nn.silu(gate) * up).astype(BF16)


def _ffn_up(h, wg, wu, layer, *, tm=2048, tn=256):
    s, d = h.shape
    f = wg.shape[2]
    w_spec = pl.BlockSpec((None, d, tn), lambda i, j: (layer, 0, j))
    return pl.pallas_call(
        _ffn_up_kernel,
        grid=(s // tm, f // tn),
        in_specs=[pl.BlockSpec((tm, d), lambda i, j: (i, 0)), w_spec, w_spec],
        out_specs=pl.BlockSpec((tm, tn), lambda i, j: (i, j)),
        out_shape=jax.ShapeDtypeStruct((s, f), BF16),
        compiler_params=_params("parallel", "arbitrary"),
        name="ffn_up",
    )(h, wg, wu)


def kernel(x, norm_mix_g, w_in, ln_a_g, ln_a_b, w_spatial, b_spatial, w_pool,
           pool_scale, conv_w, w_branch_a, w_branch_b, w_branch_c, w_out,
           norm_ffn_g, w_ffn_gate, w_ffn_up, w_ffn_down, final_norm_g):
    bsz, s, d = x.shape
    depth = w_in.shape[0]
    assert (bsz, d) == (1, D_MODEL) and w_in.shape[2] == N_IN
    xs = x.reshape(s, d)
    gw_a = D_HALF // A_GROUPS
    w_in, w_pool, w_branch_a, w_branch_b, w_branch_c, w_out, w_ffn_gate, w_ffn_up, w_ffn_down = (
        w.astype(BF16) for w in (w_in, w_pool, w_branch_a, w_branch_b, w_branch_c, w_out,
                                 w_ffn_gate, w_ffn_up, w_ffn_down))
    norm_mix_g, ln_a_g, ln_a_b, pool_scale, norm_ffn_g = (
        v[:, None, :] for v in (norm_mix_g, ln_a_g, ln_a_b, pool_scale, norm_ffn_g))
    for l in range(depth):
        h = _rms_norm_call(xs, norm_mix_g, l, BF16)
        proj = _inproj(h, w_in, l)
        bias = jnp.repeat(b_spatial[l].T, gw_a, axis=1)
        ya = _branch_a(proj, ln_a_g, ln_a_b, w_spatial, bias, l)
        yb = _branch_b(proj, w_pool, pool_scale, l)
        yc = _branch_c(proj, conv_w, l)
        merged = _merge(ya, yb, yc, w_branch_a, w_branch_b, w_branch_c, proj, l)
        xs = _matmul_residual(merged, w_out, xs, l, tm=1024, tn=1024, name="outproj")
        h = _rms_norm_call(xs, norm_ffn_g, l, BF16)
        ff = _ffn_up(h, w_ffn_gate, w_ffn_up, l)
        xs = _matmul_residual(ff, w_ffn_down, xs, l, tm=512, tn=512, name="ffn_down")
    out = _rms_norm_call(xs, final_norm_g[None, None, :], 0, F32)
    return out.reshape(bsz, s, d)
```

```python
import functools
from typing import NamedTuple

import jax
import jax.numpy as jnp
from jax import lax
from jax.experimental import pallas as pl
from jax.experimental.pallas import tpu as pltpu

D_MODEL = 4096
D_HALF = D_MODEL // 2
A_GROUPS = 8
A_CHUNK = 128
POOL_WINDOWS = (2, 4, 8, 16)
POOL_HALO = 16
CONV_HALO = 8
N_IN = 12 * D_HALF
GATE_COL0 = 6 * D_HALF
EPS = 1e-6
EPILOGUE_ROWS = 256

V7X_VMEM_LIMIT_BYTES = 56 * 1024 * 1024

F32 = jnp.float32
BF16 = jnp.bfloat16


def _params(*semantics):
    return pltpu.CompilerParams(dimension_semantics=semantics,
                                vmem_limit_bytes=V7X_VMEM_LIMIT_BYTES)


def _rms_norm(x, g):
    ms = jnp.mean(x * x, axis=-1, keepdims=True)
    return (x * lax.rsqrt(ms + EPS)) * g


def _layer_vec_spec(width, layer, grid_rank):
    if grid_rank == 1:
        return pl.BlockSpec((None, 1, width), lambda i: (layer, 0, 0))
    return pl.BlockSpec((None, 1, width), lambda i, j: (layer, 0, 0))


def _weight_cols_spec(w, layer, tn):
    if w.ndim == 2:
        return pl.BlockSpec((w.shape[0], tn), lambda i, j: (0, j))
    return pl.BlockSpec((None, w.shape[1], tn), lambda i, j: (layer, 0, j))


class CastRider(NamedTuple):
    stack: jax.Array
    layer: int


BF16_SUBLANE_TILE = 16


def _rider_block_rows(rows, n_steps):
    for rb in range(BF16_SUBLANE_TILE, rows + 1, BF16_SUBLANE_TILE):
        if rows % rb == 0 and rows // rb <= n_steps:
            return rb
    raise ValueError(f"cannot convert {rows} rows in {n_steps} grid steps")


def _rider_specs(riders, n_outer, n_inner):
    in_specs, out_specs, out_shapes = [], [], []
    for stack, layer in riders:
        _, rows, cols = stack.shape
        rb = _rider_block_rows(rows, n_outer * n_inner)
        last = rows // rb - 1

        def block(i, j, last=last):
            return jnp.minimum(i * n_inner + j, last)

        in_specs.append(pl.BlockSpec((None, rb, cols),
                                     lambda i, j, layer=layer, block=block: (layer, block(i, j), 0)))
        out_specs.append(pl.BlockSpec((rb, cols), lambda i, j, block=block: (block(i, j), 0)))
        out_shapes.append(jax.ShapeDtypeStruct((rows, cols), BF16))
    return in_specs, out_specs, out_shapes


def _split_rider_refs(refs, n_in, n_riders):
    rider_in = refs[n_in:n_in + n_riders]
    rider_out = refs[n_in + n_riders + 1:n_in + 2 * n_riders + 1]

    def ride():
        for src, dst in zip(rider_in, rider_out):
            dst[...] = src[...].astype(BF16)

    return refs[:n_in], refs[n_in + n_riders], ride


def _rms_norm_kernel(x_ref, g_ref, o_ref):
    o_ref[...] = _rms_norm(x_ref[...], g_ref[...]).astype(o_ref.dtype)


def _rms_norm_call(x, g, layer, out_dtype, *, tm=512):
    s, d = x.shape
    return pl.pallas_call(
        _rms_norm_kernel,
        grid=(s // tm,),
        in_specs=[pl.BlockSpec((tm, d), lambda i: (i, 0)),
                  _layer_vec_spec(d, layer, 1)],
        out_specs=pl.BlockSpec((tm, d), lambda i: (i, 0)),
        out_shape=jax.ShapeDtypeStruct((s, d), out_dtype),
        compiler_params=_params("parallel"),
        name="rms_norm",
    )(x, g)


def _inproj_kernel(*refs, n_riders, gelu_tiles, gate_tile0):
    (h_ref, w_ref), o_ref, ride = _split_rider_refs(refs, 2, n_riders)
    j = pl.program_id(1)

    @pl.when(j < gelu_tiles)
    def _():
        o_ref[...] = jax.nn.gelu(jnp.dot(h_ref[...], w_ref[...], preferred_element_type=F32))
        ride()

    @pl.when((j >= gelu_tiles) & (j < gate_tile0))
    def _():
        o_ref[...] = jnp.dot(h_ref[...], w_ref[...], preferred_element_type=F32)
        ride()

    @pl.when(j >= gate_tile0)
    def _():
        for r in range(0, o_ref.shape[0], EPILOGUE_ROWS):
            rows = slice(r, r + EPILOGUE_ROWS)
            o_ref[rows, :] = jax.nn.sigmoid(
                jnp.dot(h_ref[rows, :], w_ref[...], preferred_element_type=F32))
        ride()


def _inproj(h, w, layer, riders=(), *, tm=1024, tn=1024):
    s, d = h.shape
    n = w.shape[-1]
    nj = n // tn
    kern = functools.partial(_inproj_kernel, n_riders=len(riders),
                             gelu_tiles=(2 * D_HALF) // tn, gate_tile0=GATE_COL0 // tn)
    r_in, r_out, r_shapes = _rider_specs(riders, s // tm, nj)
    outs = pl.pallas_call(
        kern,
        grid=(s // tm, nj),
        in_specs=[pl.BlockSpec((tm, d), lambda i, j: (i, 0)),
                  _weight_cols_spec(w, layer, tn)] + r_in,
        out_specs=[pl.BlockSpec((tm, tn), lambda i, j: (i, j))] + r_out,
        out_shape=[jax.ShapeDtypeStruct((s, n), F32)] + r_shapes,
        compiler_params=_params("arbitrary", "arbitrary"),
        name="inproj",
    )(h, w, *(r.stack for r in riders))
    return outs[0], outs[1:]


def _branch_a_kernel(u_ref, v_ref, lng_ref, lnb_ref, ws_ref, bias_ref, y_ref):
    tm = u_ref.shape[0]
    gw = D_HALF // A_GROUPS
    v = v_ref[...]
    mu = jnp.mean(v, axis=-1, keepdims=True)
    dv = v - mu
    var = jnp.mean(dv * dv, axis=-1, keepdims=True)
    vn = ((dv * lax.rsqrt(var + EPS)) * lng_ref[...] + lnb_ref[...]).astype(BF16)
    t_idx = lax.broadcasted_iota(jnp.int32, (A_CHUNK, A_CHUNK), 0)
    s_idx = lax.broadcasted_iota(jnp.int32, (A_CHUNK, A_CHUNK), 1)
    causal = s_idx <= t_idx
    for g in range(A_GROUPS):
        wg = jnp.where(causal, ws_ref[g], 0.0).astype(BF16)
        cols = slice(g * gw, (g + 1) * gw)
        for n in range(tm // A_CHUNK):
            rows = slice(n * A_CHUNK, (n + 1) * A_CHUNK)
            mixed = jnp.dot(wg, vn[rows, cols], preferred_element_type=F32)
            mixed = mixed + bias_ref[:, cols]
            y_ref[rows, cols] = (u_ref[rows, cols] * mixed).astype(BF16)


def _branch_a(proj, ln_g, ln_b, w_s, bias, layer, *, tm=512):
    s = proj.shape[0]
    return pl.pallas_call(
        _branch_a_kernel,
        grid=(s // tm,),
        in_specs=[pl.BlockSpec((tm, D_HALF), lambda i: (i, 0)),
                  pl.BlockSpec((tm, D_HALF), lambda i: (i, 1)),
                  _layer_vec_spec(D_HALF, layer, 1),
                  _layer_vec_spec(D_HALF, layer, 1),
                  pl.BlockSpec((None, A_GROUPS, A_CHUNK, A_CHUNK),
                               lambda i: (layer, 0, 0, 0)),
                  pl.BlockSpec((A_CHUNK, D_HALF), lambda i: (0, 0))],
        out_specs=pl.BlockSpec((tm, D_HALF), lambda i: (i, 0)),
        out_shape=jax.ShapeDtypeStruct((s, D_HALF), BF16),
        compiler_params=_params("parallel"),
        name="branch_a",
    )(proj, proj, ln_g, ln_b, w_s, bias)


def _branch_b_kernel(x_ref, halo_ref, wp_ref, scale_ref, y_ref):
    i = pl.program_id(0)
    tm = x_ref.shape[0]
    gw = D_HALF // len(POOL_WINDOWS)
    pos = i * tm + lax.broadcasted_iota(jnp.int32, (tm, 1), 0) + 1
    for g, win in enumerate(POOL_WINDOWS):
        cols = slice(g * gw, (g + 1) * gw)
        x = x_ref[:, cols]
        ext = jnp.concatenate([jnp.where(i > 0, halo_ref[:, cols], 0.0), x], axis=0)
        span = 1
        acc = ext
        while span < win:
            acc = acc[span:, :] + acc[:-span, :]
            span *= 2
        wsum = acc[POOL_HALO - (win - 1):, :]
        count = jnp.minimum(pos, win).astype(F32)
        pooled = (wsum / count - x).astype(BF16)
        y = jnp.dot(pooled, wp_ref[g], preferred_element_type=F32)
        y_ref[:, cols] = (y * scale_ref[:, cols]).astype(BF16)


def _branch_b(proj, w_pool, scale, layer, *, tm=512):
    s = proj.shape[0]
    gw = D_HALF // len(POOL_WINDOWS)
    hb = tm // POOL_HALO
    return pl.pallas_call(
        _branch_b_kernel,
        grid=(s // tm,),
        in_specs=[pl.BlockSpec((tm, D_HALF), lambda i: (i, 2)),
                  pl.BlockSpec((POOL_HALO, D_HALF),
                               lambda i: (jnp.maximum(i * hb - 1, 0), 2)),
                  pl.BlockSpec((None, len(POOL_WINDOWS), gw, gw),
                               lambda i: (layer, 0, 0, 0)),
                  _layer_vec_spec(D_HALF, layer, 1)],
        out_specs=pl.BlockSpec((tm, D_HALF), lambda i: (i, 0)),
        out_shape=jax.ShapeDtypeStruct((s, D_HALF), BF16),
        compiler_params=_params("parallel"),
        name="branch_b",
    )(proj, proj, w_pool, scale)


def _branch_c_kernel(b_ref, c_ref, h_ref, ch_ref, hh_ref, cw_ref, y_ref):
    i = pl.program_id(0)
    tm = b_ref.shape[0]
    z = c_ref[...] * h_ref[...]
    zh = jnp.where(i > 0, ch_ref[...] * hh_ref[...], 0.0)
    ext = jnp.concatenate([zh, z], axis=0)
    conv = (cw_ref[0:1, :] * ext[CONV_HALO - 2:CONV_HALO - 2 + tm, :]
            + cw_ref[1:2, :] * ext[CONV_HALO - 1:CONV_HALO - 1 + tm, :]
            + cw_ref[2:3, :] * z)
    y_ref[...] = (b_ref[...] * conv).astype(BF16)


def _branch_c(proj, conv_w, layer, *, tm=512):
    s = proj.shape[0]
    hb = tm // CONV_HALO

    def halo_spec(col):
        return pl.BlockSpec((CONV_HALO, D_HALF),
                            lambda i: (jnp.maximum(i * hb - 1, 0), col))

    return pl.pallas_call(
        _branch_c_kernel,
        grid=(s // tm,),
        in_specs=[pl.BlockSpec((tm, D_HALF), lambda i: (i, 3)),
                  pl.BlockSpec((tm, D_HALF), lambda i: (i, 4)),
                  pl.BlockSpec((tm, D_HALF), lambda i: (i, 5)),
                  halo_spec(4), halo_spec(5),
                  pl.BlockSpec((None, 3, D_HALF), lambda i: (layer, 0, 0))],
        out_specs=pl.BlockSpec((tm, D_HALF), lambda i: (i, 0)),
        out_shape=jax.ShapeDtypeStruct((s, D_HALF), BF16),
        compiler_params=_params("parallel"),
        name="branch_c",
    )(proj, proj, proj, proj, proj, conv_w)


def _merge_kernel(ya_ref, yb_ref, yc_ref, wa_ref, wb_ref, wc_ref,
                  ga_ref, gb_ref, gc_ref, o_ref):
    m = ga_ref[...] * jnp.dot(ya_ref[...], wa_ref[...], preferred_element_type=F32)
    m = m + gb_ref[...] * jnp.dot(yb_ref[...], wb_ref[...], preferred_element_type=F32)
    m = m + gc_ref[...] * jnp.dot(yc_ref[...], wc_ref[...], preferred_element_type=F32)
    o_ref[...] = m.astype(BF16)


def _merge(ya, yb, yc, wa, wb, wc, proj, layer, *, tm=1024, tn=512):
    s = ya.shape[0]
    g0 = GATE_COL0 // tn
    gstep = D_MODEL // tn
    y_spec = pl.BlockSpec((tm, D_HALF), lambda i, j: (i, 0))
    w_spec = _weight_cols_spec(wa, layer, tn)

    def gate_spec(k):
        return pl.BlockSpec((tm, tn), lambda i, j: (i, g0 + k * gstep + j))

    return pl.pallas_call(
        _merge_kernel,
        grid=(s // tm, D_MODEL // tn),
        in_specs=[y_spec, y_spec, y_spec, w_spec, w_spec, w_spec,
                  gate_spec(0), gate_spec(1), gate_spec(2)],
        out_specs=pl.BlockSpec((tm, tn), lambda i, j: (i, j)),
        out_shape=jax.ShapeDtypeStruct((s, D_MODEL), BF16),
        compiler_params=_params("parallel", "arbitrary"),
        name="merge",
    )(ya, yb, yc, wa, wb, wc, proj, proj, proj)


def _matmul_residual_kernel(a_ref, w_ref, x_ref, o_ref):
    o_ref[...] = x_ref[...] + jnp.dot(a_ref[...], w_ref[...], preferred_element_type=F32)


def _matmul_residual(a, w, x, layer, *, tm, tn, name):
    s, k = a.shape
    n = w.shape[-1]
    return pl.pallas_call(
        _matmul_residual_kernel,
        grid=(s // tm, n // tn),
        in_specs=[pl.BlockSpec((tm, k), lambda i, j: (i, 0)),
                  _weight_cols_spec(w, layer, tn),
                  pl.BlockSpec((tm, tn), lambda i, j: (i, j))],
        out_specs=pl.BlockSpec((tm, tn), lambda i, j: (i, j)),
        out_shape=jax.ShapeDtypeStruct((s, n), F32),
        compiler_params=_params("parallel", "arbitrary"),
        name=name,
    )(a, w, x)


def _ffn_up_kernel(*refs, n_riders):
    (h_ref, wg_ref, wu_ref), o_ref, ride = _split_rider_refs(refs, 3, n_riders)
    h = h_ref[...]
    gate = jnp.dot(h, wg_ref[...], preferred_element_type=F32)
    up = jnp.dot(h, wu_ref[...], preferred_element_type=F32)
    o_ref[...] = (jax.nn.silu(gate) * up).astype(BF16)
    ride()


def _ffn_up(h, wg, wu, layer, riders=(), *, tm=2048, tn=256):
    s, d = h.shape
    f = wg.shape[-1]
    nj = f // tn
    r_in, r_out, r_shapes = _rider_specs(riders, s // tm, nj)
    outs = pl.pallas_call(
        functools.partial(_ffn_up_kernel, n_riders=len(riders)),
        grid=(s // tm, nj),
        in_specs=[pl.BlockSpec((tm, d), lambda i, j: (i, 0)),
                  _weight_cols_spec(wg, layer, tn),
                  _weight_cols_spec(wu, layer, tn)] + r_in,
        out_specs=[pl.BlockSpec((tm, tn), lambda i, j: (i, j))] + r_out,
        out_shape=[jax.ShapeDtypeStruct((s, f), BF16)] + r_shapes,
        compiler_params=_params("arbitrary", "arbitrary"),
        name="ffn_up",
    )(h, wg, wu, *(r.stack for r in riders))
    return outs[0], outs[1:]


def kernel(x, norm_mix_g, w_in, ln_a_g, ln_a_b, w_spatial, b_spatial, w_pool,
           pool_scale, conv_w, w_branch_a, w_branch_b, w_branch_c, w_out,
           norm_ffn_g, w_ffn_gate, w_ffn_up, w_ffn_down, final_norm_g):
    bsz, s, d = x.shape
    depth = w_in.shape[0]
    assert (bsz, d) == (1, D_MODEL) and w_in.shape[2] == N_IN
    xs = x.reshape(s, d)
    gw_a = D_HALF // A_GROUPS
    w_in, w_pool, w_branch_a, w_branch_b, w_branch_c, w_out = (
        w.astype(BF16) for w in (w_in, w_pool, w_branch_a, w_branch_b, w_branch_c, w_out))
    norm_mix_g, ln_a_g, ln_a_b, pool_scale, norm_ffn_g = (
        v[:, None, :] for v in (norm_mix_g, ln_a_g, ln_a_b, pool_scale, norm_ffn_g))

    def ffn_riders(layer):
        return (CastRider(w_ffn_gate, layer), CastRider(w_ffn_up, layer),
                CastRider(w_ffn_down, layer))

    ffn_w = None
    for l in range(depth):
        h = _rms_norm_call(xs, norm_mix_g, l, BF16)
        proj, cast = _inproj(h, w_in, l, ffn_riders(0) if l == 0 else ())
        if l == 0:
            ffn_w = cast
        bias = jnp.repeat(b_spatial[l].T, gw_a, axis=1)
        ya = _branch_a(proj, ln_a_g, ln_a_b, w_spatial, bias, l)
        yb = _branch_b(proj, w_pool, pool_scale, l)
        yc = _branch_c(proj, conv_w, l)
        merged = _merge(ya, yb, yc, w_branch_a, w_branch_b, w_branch_c, proj, l)
        xs = _matmul_residual(merged, w_out, xs, l, tm=1024, tn=1024, name="outproj")
        h = _rms_norm_call(xs, norm_ffn_g, l, BF16)
        wg, wu, wd = ffn_w
        ff, ffn_w = _ffn_up(h, wg, wu, l, ffn_riders(l + 1) if l + 1 < depth else ())
        xs = _matmul_residual(ff, wd, xs, l, tm=512, tn=512, name="ffn_down")
    out = _rms_norm_call(xs, final_norm_g[None, None, :], 0, F32)
    return out.reshape(bsz, s, d)
```

```python
import functools
from typing import NamedTuple

import jax
import jax.numpy as jnp
from jax import lax
from jax.experimental import pallas as pl
from jax.experimental.pallas import tpu as pltpu

D_MODEL = 4096
D_HALF = D_MODEL // 2
A_GROUPS = 8
A_CHUNK = 128
POOL_WINDOWS = (2, 4, 8, 16)
POOL_HALO = 16
CONV_HALO = 8
N_IN = 12 * D_HALF
GATE_COL0 = 6 * D_HALF
EPS = 1e-6
EPILOGUE_ROWS = 256
LANES = 128
BF16_SUBLANE_TILE = 16

V7X_VMEM_LIMIT_BYTES = 56 * 1024 * 1024

F32 = jnp.float32
BF16 = jnp.bfloat16


def _params(*semantics):
    return pltpu.CompilerParams(dimension_semantics=semantics,
                                vmem_limit_bytes=V7X_VMEM_LIMIT_BYTES)


def _layer_vec_spec(width, layer, grid_rank):
    if grid_rank == 1:
        return pl.BlockSpec((None, 1, width), lambda i: (layer, 0, 0))
    return pl.BlockSpec((None, 1, width), lambda i, j: (layer, 0, 0))


def _weight_cols_spec(w, layer, tn):
    if w.ndim == 2:
        return pl.BlockSpec((w.shape[0], tn), lambda i, j: (0, j))
    return pl.BlockSpec((None, w.shape[1], tn), lambda i, j: (layer, 0, j))


def _lane_partial_sumsq(x):
    sq = x * x
    acc = sq[:, :LANES]
    for c in range(LANES, x.shape[1], LANES):
        acc = acc + sq[:, c:c + LANES]
    return acc


def _row_rms_scale(ss_ref):
    ms = jnp.sum(ss_ref[...], axis=-1, keepdims=True) * (1.0 / D_MODEL)
    return lax.rsqrt(ms + EPS)


class CastRider(NamedTuple):
    stack: jax.Array
    layer: int


def _rider_block_rows(rows, n_steps):
    for rb in range(BF16_SUBLANE_TILE, rows + 1, BF16_SUBLANE_TILE):
        if rows % rb == 0 and rows // rb <= n_steps:
            return rb
    raise ValueError(f"cannot convert {rows} rows in {n_steps} grid steps")


def _rider_specs(riders, n_outer, n_inner):
    in_specs, out_specs, out_shapes = [], [], []
    for stack, layer in riders:
        _, rows, cols = stack.shape
        rb = _rider_block_rows(rows, n_outer * n_inner)
        last = rows // rb - 1

        def block(i, j, last=last):
            return jnp.minimum(i * n_inner + j, last)

        in_specs.append(pl.BlockSpec((None, rb, cols),
                                     lambda i, j, layer=layer, block=block: (layer, block(i, j), 0)))
        out_specs.append(pl.BlockSpec((rb, cols), lambda i, j, block=block: (block(i, j), 0)))
        out_shapes.append(jax.ShapeDtypeStruct((rows, cols), BF16))
    return in_specs, out_specs, out_shapes


def _split_refs(refs, n_in, n_out, n_riders):
    rider_in = refs[n_in:n_in + n_riders]
    out0 = n_in + n_riders
    rider_out = refs[out0 + n_out:out0 + n_out + n_riders]

    def ride():
        for src, dst in zip(rider_in, rider_out):
            dst[...] = src[...].astype(BF16)

    return refs[:n_in], refs[out0:out0 + n_out], ride


def _prescale_kernel(x_ref, g_ref, xg_ref, ss_ref):
    x = x_ref[...]
    xg_ref[...] = (x * g_ref[...]).astype(BF16)
    ss_ref[...] = _lane_partial_sumsq(x)


def _prescale(x, g, layer, *, tm=512):
    s, d = x.shape
    return pl.pallas_call(
        _prescale_kernel,
        grid=(s // tm,),
        in_specs=[pl.BlockSpec((tm, d), lambda i: (i, 0)),
                  _layer_vec_spec(d, layer, 1)],
        out_specs=[pl.BlockSpec((tm, d), lambda i: (i, 0)),
                   pl.BlockSpec((tm, LANES), lambda i: (i, 0))],
        out_shape=[jax.ShapeDtypeStruct((s, d), BF16),
                   jax.ShapeDtypeStruct((s, LANES), F32)],
        compiler_params=_params("parallel"),
        name="prescale",
    )(x, g)


def _final_norm_kernel(x_ref, g_ref, o_ref):
    x = x_ref[...]
    ms = jnp.mean(x * x, axis=-1, keepdims=True)
    o_ref[...] = (x * lax.rsqrt(ms + EPS)) * g_ref[...]


def _final_norm(x, g, *, tm=512):
    s, d = x.shape
    return pl.pallas_call(
        _final_norm_kernel,
        grid=(s // tm,),
        in_specs=[pl.BlockSpec((tm, d), lambda i: (i, 0)),
                  pl.BlockSpec((1, d), lambda i: (0, 0))],
        out_specs=pl.BlockSpec((tm, d), lambda i: (i, 0)),
        out_shape=jax.ShapeDtypeStruct((s, d), F32),
        compiler_params=_params("parallel"),
        name="final_norm",
    )(x, g)


def _inproj_kernel(*refs, n_riders, gelu_tiles, gate_tile0):
    (xg_ref, ss_ref, w_ref), (o_ref,), ride = _split_refs(refs, 3, 1, n_riders)
    j = pl.program_id(1)
    r = _row_rms_scale(ss_ref)

    def project(epilogue):
        for r0 in range(0, o_ref.shape[0], EPILOGUE_ROWS):
            rows = slice(r0, r0 + EPILOGUE_ROWS)
            o_ref[rows, :] = epilogue(
                r[rows] * jnp.dot(xg_ref[rows, :], w_ref[...], preferred_element_type=F32))
        ride()

    @pl.when(j < gelu_tiles)
    def _():
        project(jax.nn.gelu)

    @pl.when((j >= gelu_tiles) & (j < gate_tile0))
    def _():
        project(lambda z: z)

    @pl.when(j >= gate_tile0)
    def _():
        project(jax.nn.sigmoid)


def _inproj(xg, ss, w, layer, riders=(), *, tm=1024, tn=1024):
    s, d = xg.shape
    n = w.shape[-1]
    nj = n // tn
    kern = functools.partial(_inproj_kernel, n_riders=len(riders),
                             gelu_tiles=(2 * D_HALF) // tn, gate_tile0=GATE_COL0 // tn)
    r_in, r_out, r_shapes = _rider_specs(riders, s // tm, nj)
    outs = pl.pallas_call(
        kern,
        grid=(s // tm, nj),
        in_specs=[pl.BlockSpec((tm, d), lambda i, j: (i, 0)),
                  pl.BlockSpec((tm, LANES), lambda i, j: (i, 0)),
                  _weight_cols_spec(w, layer, tn)] + r_in,
        out_specs=[pl.BlockSpec((tm, tn), lambda i, j: (i, j))] + r_out,
        out_shape=[jax.ShapeDtypeStruct((s, n), F32)] + r_shapes,
        compiler_params=_params("arbitrary", "arbitrary"),
        name="inproj",
    )(xg, ss, w, *(r.stack for r in riders))
    return outs[0], outs[1:]


def _branch_a_kernel(u_ref, v_ref, lng_ref, lnb_ref, ws_ref, bias_ref, y_ref):
    tm = u_ref.shape[0]
    gw = D_HALF // A_GROUPS
    v = v_ref[...]
    mu = jnp.mean(v, axis=-1, keepdims=True)
    dv = v - mu
    var = jnp.mean(dv * dv, axis=-1, keepdims=True)
    vn = ((dv * lax.rsqrt(var + EPS)) * lng_ref[...] + lnb_ref[...]).astype(BF16)
    t_idx = lax.broadcasted_iota(jnp.int32, (A_CHUNK, A_CHUNK), 0)
    s_idx = lax.broadcasted_iota(jnp.int32, (A_CHUNK, A_CHUNK), 1)
    causal = s_idx <= t_idx
    for g in range(A_GROUPS):
        wg = jnp.where(causal, ws_ref[g], 0.0).astype(BF16)
        cols = slice(g * gw, (g + 1) * gw)
        for n in range(tm // A_CHUNK):
            rows = slice(n * A_CHUNK, (n + 1) * A_CHUNK)
            mixed = jnp.dot(wg, vn[rows, cols], preferred_element_type=F32)
            mixed = mixed + bias_ref[:, cols]
            y_ref[rows, cols] = (u_ref[rows, cols] * mixed).astype(BF16)


def _branch_a(proj, ln_g, ln_b, w_s, bias, layer, *, tm=512):
    s = proj.shape[0]
    return pl.pallas_call(
        _branch_a_kernel,
        grid=(s // tm,),
        in_specs=[pl.BlockSpec((tm, D_HALF), lambda i: (i, 0)),
                  pl.BlockSpec((tm, D_HALF), lambda i: (i, 1)),
                  _layer_vec_spec(D_HALF, layer, 1),
                  _layer_vec_spec(D_HALF, layer, 1),
                  pl.BlockSpec((None, A_GROUPS, A_CHUNK, A_CHUNK),
                               lambda i: (layer, 0, 0, 0)),
                  pl.BlockSpec((A_CHUNK, D_HALF), lambda i: (0, 0))],
        out_specs=pl.BlockSpec((tm, D_HALF), lambda i: (i, 0)),
        out_shape=jax.ShapeDtypeStruct((s, D_HALF), BF16),
        compiler_params=_params("parallel"),
        name="branch_a",
    )(proj, proj, ln_g, ln_b, w_s, bias)


def _branch_b_kernel(x_ref, halo_ref, wp_ref, scale_ref, y_ref):
    i = pl.program_id(0)
    tm = x_ref.shape[0]
    gw = D_HALF // len(POOL_WINDOWS)
    pos = i * tm + lax.broadcasted_iota(jnp.int32, (tm, 1), 0) + 1
    for g, win in enumerate(POOL_WINDOWS):
        cols = slice(g * gw, (g + 1) * gw)
        x = x_ref[:, cols]
        ext = jnp.concatenate([jnp.where(i > 0, halo_ref[:, cols], 0.0), x], axis=0)
        span = 1
        acc = ext
        while span < win:
            acc = acc[span:, :] + acc[:-span, :]
            span *= 2
        wsum = acc[POOL_HALO - (win - 1):, :]
        count = jnp.minimum(pos, win).astype(F32)
        pooled = (wsum / count - x).astype(BF16)
        y = jnp.dot(pooled, wp_ref[g], preferred_element_type=F32)
        y_ref[:, cols] = (y * scale_ref[:, cols]).astype(BF16)


def _branch_b(proj, w_pool, scale, layer, *, tm=512):
    s = proj.shape[0]
    gw = D_HALF // len(POOL_WINDOWS)
    hb = tm // POOL_HALO
    return pl.pallas_call(
        _branch_b_kernel,
        grid=(s // tm,),
        in_specs=[pl.BlockSpec((tm, D_HALF), lambda i: (i, 2)),
                  pl.BlockSpec((POOL_HALO, D_HALF),
                               lambda i: (jnp.maximum(i * hb - 1, 0), 2)),
                  pl.BlockSpec((None, len(POOL_WINDOWS), gw, gw),
                               lambda i: (layer, 0, 0, 0)),
                  _layer_vec_spec(D_HALF, layer, 1)],
        out_specs=pl.BlockSpec((tm, D_HALF), lambda i: (i, 0)),
        out_shape=jax.ShapeDtypeStruct((s, D_HALF), BF16),
        compiler_params=_params("parallel"),
        name="branch_b",
    )(proj, proj, w_pool, scale)


def _branch_c_kernel(b_ref, c_ref, h_ref, ch_ref, hh_ref, cw_ref, y_ref):
    i = pl.program_id(0)
    tm = b_ref.shape[0]
    z = c_ref[...] * h_ref[...]
    zh = jnp.where(i > 0, ch_ref[...] * hh_ref[...], 0.0)
    ext = jnp.concatenate([zh, z], axis=0)
    conv = (cw_ref[0:1, :] * ext[CONV_HALO - 2:CONV_HALO - 2 + tm, :]
            + cw_ref[1:2, :] * ext[CONV_HALO - 1:CONV_HALO - 1 + tm, :]
            + cw_ref[2:3, :] * z)
    y_ref[...] = (b_ref[...] * conv).astype(BF16)


def _branch_c(proj, conv_w, layer, *, tm=512):
    s = proj.shape[0]
    hb = tm // CONV_HALO

    def halo_spec(col):
        return pl.BlockSpec((CONV_HALO, D_HALF),
                            lambda i: (jnp.maximum(i * hb - 1, 0), col))

    return pl.pallas_call(
        _branch_c_kernel,
        grid=(s // tm,),
        in_specs=[pl.BlockSpec((tm, D_HALF), lambda i: (i, 3)),
                  pl.BlockSpec((tm, D_HALF), lambda i: (i, 4)),
                  pl.BlockSpec((tm, D_HALF), lambda i: (i, 5)),
                  halo_spec(4), halo_spec(5),
                  pl.BlockSpec((None, 3, D_HALF), lambda i: (layer, 0, 0))],
        out_specs=pl.BlockSpec((tm, D_HALF), lambda i: (i, 0)),
        out_shape=jax.ShapeDtypeStruct((s, D_HALF), BF16),
        compiler_params=_params("parallel"),
        name="branch_c",
    )(proj, proj, proj, proj, proj, conv_w)


def _merge_kernel(ya_ref, yb_ref, yc_ref, wa_ref, wb_ref, wc_ref,
                  ga_ref, gb_ref, gc_ref, o_ref):
    m = ga_ref[...] * jnp.dot(ya_ref[...], wa_ref[...], preferred_element_type=F32)
    m = m + gb_ref[...] * jnp.dot(yb_ref[...], wb_ref[...], preferred_element_type=F32)
    m = m + gc_ref[...] * jnp.dot(yc_ref[...], wc_ref[...], preferred_element_type=F32)
    o_ref[...] = m.astype(BF16)


def _merge(ya, yb, yc, wa, wb, wc, proj, layer, *, tm=1024, tn=512):
    s = ya.shape[0]
    g0 = GATE_COL0 // tn
    gstep = D_MODEL // tn
    y_spec = pl.BlockSpec((tm, D_HALF), lambda i, j: (i, 0))
    w_spec = _weight_cols_spec(wa, layer, tn)

    def gate_spec(k):
        return pl.BlockSpec((tm, tn), lambda i, j: (i, g0 + k * gstep + j))

    return pl.pallas_call(
        _merge_kernel,
        grid=(s // tm, D_MODEL // tn),
        in_specs=[y_spec, y_spec, y_spec, w_spec, w_spec, w_spec,
                  gate_spec(0), gate_spec(1), gate_spec(2)],
        out_specs=pl.BlockSpec((tm, tn), lambda i, j: (i, j)),
        out_shape=jax.ShapeDtypeStruct((s, D_MODEL), BF16),
        compiler_params=_params("parallel", "arbitrary"),
        name="merge",
    )(ya, yb, yc, wa, wb, wc, proj, proj, proj)


def _matmul_residual_kernel(*refs, n_riders, emit_norm_operand):
    if emit_norm_operand:
        (a_ref, w_ref, x_ref, g_ref), (o_ref, xg_ref, ss_ref), ride = _split_refs(
            refs, 4, 3, n_riders)
    else:
        (a_ref, w_ref, x_ref), (o_ref,), ride = _split_refs(refs, 3, 1, n_riders)
    if emit_norm_operand:
        @pl.when(pl.program_id(1) == 0)
        def _():
            ss_ref[...] = jnp.zeros_like(ss_ref)

    for r0 in range(0, o_ref.shape[0], EPILOGUE_ROWS):
        rows = slice(r0, r0 + EPILOGUE_ROWS)
        y = x_ref[rows, :] + jnp.dot(a_ref[rows, :], w_ref[...], preferred_element_type=F32)
        o_ref[rows, :] = y
        if emit_norm_operand:
            xg_ref[rows, :] = (y * g_ref[...]).astype(BF16)
            ss_ref[rows, :] += _lane_partial_sumsq(y)
    ride()


def _matmul_residual(a, w, x, layer, norm_g=None, norm_layer=None, riders=(), *, tm, tn, name):
    s, k = a.shape
    n = w.shape[-1]
    nj = n // tn
    emit = norm_g is not None
    tile = pl.BlockSpec((tm, tn), lambda i, j: (i, j))
    in_specs = [pl.BlockSpec((tm, k), lambda i, j: (i, 0)), _weight_cols_spec(w, layer, tn), tile]
    out_specs = [tile]
    out_shape = [jax.ShapeDtypeStruct((s, n), F32)]
    args = [a, w, x]
    if emit:
        in_specs.append(pl.BlockSpec((None, 1, tn), lambda i, j: (norm_layer, 0, j)))
        args.append(norm_g)
        out_specs += [tile, pl.BlockSpec((tm, LANES), lambda i, j: (i, 0))]
        out_shape += [jax.ShapeDtypeStruct((s, n), BF16), jax.ShapeDtypeStruct((s, LANES), F32)]
    r_in, r_out, r_shapes = _rider_specs(riders, s // tm, nj)
    outs = pl.pallas_call(
        functools.partial(_matmul_residual_kernel, n_riders=len(riders), emit_norm_operand=emit),
        grid=(s // tm, nj),
        in_specs=in_specs + r_in,
        out_specs=out_specs + r_out,
        out_shape=out_shape + r_shapes,
        compiler_params=_params("arbitrary", "arbitrary"),
        name=name,
    )(*args, *(r.stack for r in riders))
    n_main = len(out_shape)
    return outs[:n_main], outs[n_main:]


def _ffn_up_kernel(*refs, n_riders):
    (xg_ref, ss_ref, wg_ref, wu_ref), (o_ref,), ride = _split_refs(refs, 4, 1, n_riders)
    r = _row_rms_scale(ss_ref)
    for r0 in range(0, o_ref.shape[0], EPILOGUE_ROWS):
        rows = slice(r0, r0 + EPILOGUE_ROWS)
        xg = xg_ref[rows, :]
        gate = r[rows] * jnp.dot(xg, wg_ref[...], preferred_element_type=F32)
        up = r[rows] * jnp.dot(xg, wu_ref[...], preferred_element_type=F32)
        o_ref[rows, :] = (jax.nn.silu(gate) * up).astype(BF16)
    ride()


def _ffn_up(xg, ss, wg, wu, layer, riders=(), *, tm=2048, tn=256):
    s, d = xg.shape
    f = wg.shape[-1]
    nj = f // tn
    r_in, r_out, r_shapes = _rider_specs(riders, s // tm, nj)
    outs = pl.pallas_call(
        functools.partial(_ffn_up_kernel, n_riders=len(riders)),
        grid=(s // tm, nj),
        in_specs=[pl.BlockSpec((tm, d), lambda i, j: (i, 0)),
                  pl.BlockSpec((tm, LANES), lambda i, j: (i, 0)),
                  _weight_cols_spec(wg, layer, tn),
                  _weight_cols_spec(wu, layer, tn)] + r_in,
        out_specs=[pl.BlockSpec((tm, tn), lambda i, j: (i, j))] + r_out,
        out_shape=[jax.ShapeDtypeStruct((s, f), BF16)] + r_shapes,
        compiler_params=_params("arbitrary", "arbitrary"),
        name="ffn_up",
    )(xg, ss, wg, wu, *(r.stack for r in riders))
    return outs[0], outs[1:]


def kernel(x, norm_mix_g, w_in, ln_a_g, ln_a_b, w_spatial, b_spatial, w_pool,
           pool_scale, conv_w, w_branch_a, w_branch_b, w_branch_c, w_out,
           norm_ffn_g, w_ffn_gate, w_ffn_up, w_ffn_down, final_norm_g):
    bsz, s, d = x.shape
    depth = w_in.shape[0]
    assert (bsz, d) == (1, D_MODEL) and w_in.shape[2] == N_IN
    xs = x.reshape(s, d)
    gw_a = D_HALF // A_GROUPS
    w_pool, w_branch_a, w_branch_b, w_branch_c, w_out = (
        w.astype(BF16) for w in (w_pool, w_branch_a, w_branch_b, w_branch_c, w_out))
    norm_mix_g, ln_a_g, ln_a_b, pool_scale, norm_ffn_g = (
        v[:, None, :] for v in (norm_mix_g, ln_a_g, ln_a_b, pool_scale, norm_ffn_g))

    def ffn_riders(layer):
        return (CastRider(w_ffn_gate, layer), CastRider(w_ffn_up, layer),
                CastRider(w_ffn_down, layer))

    w_in_l = w_in[0].astype(BF16)
    xg, ss = _prescale(xs, norm_mix_g, 0)
    ffn_w = None
    for l in range(depth):
        last = l + 1 == depth
        proj, cast = _inproj(xg, ss, w_in_l, l, ffn_riders(0) if l == 0 else ())
        if l == 0:
            ffn_w = cast
        bias = jnp.repeat(b_spatial[l].T, gw_a, axis=1)
        ya = _branch_a(proj, ln_a_g, ln_a_b, w_spatial, bias, l)
        yb = _branch_b(proj, w_pool, pool_scale, l)
        yc = _branch_c(proj, conv_w, l)
        merged = _merge(ya, yb, yc, w_branch_a, w_branch_b, w_branch_c, proj, l)
        (xs, xg, ss), _ = _matmul_residual(merged, w_out, xs, l, norm_ffn_g, l,
                                           tm=1024, tn=512, name="outproj")
        wg, wu, wd = ffn_w
        ff, ffn_w = _ffn_up(xg, ss, wg, wu, l, () if last else ffn_riders(l + 1))
        if last:
            (xs,), _ = _matmul_residual(ff, wd, xs, l, tm=512, tn=512, name="ffn_down")
        else:
            (xs, xg, ss), (w_in_l,) = _matmul_residual(
                ff, wd, xs, l, norm_mix_g, l + 1, (CastRider(w_in, l + 1),),
                tm=512, tn=512, name="ffn_down")
    return _final_norm(xs, final_norm_g[None, :]).reshape(bsz, s, d)
```

```python
import functools
from typing import NamedTuple

import jax
import jax.numpy as jnp
from jax import lax
from jax.experimental import pallas as pl
from jax.experimental.pallas import tpu as pltpu

D_MODEL = 4096
D_HALF = D_MODEL // 2
A_GROUPS = 8
A_CHUNK = 128
POOL_WINDOWS = (2, 4, 8, 16)
POOL_HALO = 16
CONV_HALO = 8
N_IN = 12 * D_HALF
GATE_COL0 = 6 * D_HALF
EPS = 1e-6
EPILOGUE_ROWS = 256
MIXER_ROWS = 256
LANES = 128
BF16_SUBLANE_TILE = 16

V7X_VMEM_LIMIT_BYTES = 56 * 1024 * 1024

F32 = jnp.float32
BF16 = jnp.bfloat16


def _params(*semantics):
    return pltpu.CompilerParams(dimension_semantics=semantics,
                                vmem_limit_bytes=V7X_VMEM_LIMIT_BYTES)


def _layer_vec_spec(width, layer, grid_rank):
    if grid_rank == 1:
        return pl.BlockSpec((None, 1, width), lambda i: (layer, 0, 0))
    return pl.BlockSpec((None, 1, width), lambda i, j: (layer, 0, 0))


def _weight_cols_spec(w, layer, tn):
    if w.ndim == 2:
        return pl.BlockSpec((w.shape[0], tn), lambda i, j: (0, j))
    return pl.BlockSpec((None, w.shape[1], tn), lambda i, j: (layer, 0, j))


def _lane_partial_sumsq(x):
    sq = x * x
    acc = sq[:, :LANES]
    for c in range(LANES, x.shape[1], LANES):
        acc = acc + sq[:, c:c + LANES]
    return acc


def _row_rms_scale(ss_ref):
    ms = jnp.sum(ss_ref[...], axis=-1, keepdims=True) * (1.0 / D_MODEL)
    return lax.rsqrt(ms + EPS)


class CastRider(NamedTuple):
    stack: jax.Array
    layer: int


def _rider_block_rows(rows, n_steps):
    for rb in range(BF16_SUBLANE_TILE, rows + 1, BF16_SUBLANE_TILE):
        if rows % rb == 0 and rows // rb <= n_steps:
            return rb
    raise ValueError(f"cannot convert {rows} rows in {n_steps} grid steps")


def _rider_specs(riders, n_outer, n_inner):
    in_specs, out_specs, out_shapes = [], [], []
    for stack, layer in riders:
        _, rows, cols = stack.shape
        rb = _rider_block_rows(rows, n_outer * n_inner)
        last = rows // rb - 1

        def block(i, j, last=last):
            return jnp.minimum(i * n_inner + j, last)

        in_specs.append(pl.BlockSpec((None, rb, cols),
                                     lambda i, j, layer=layer, block=block: (layer, block(i, j), 0)))
        out_specs.append(pl.BlockSpec((rb, cols), lambda i, j, block=block: (block(i, j), 0)))
        out_shapes.append(jax.ShapeDtypeStruct((rows, cols), BF16))
    return in_specs, out_specs, out_shapes


def _split_refs(refs, n_in, n_out, n_riders, n_scratch=0):
    rider_in = refs[n_in:n_in + n_riders]
    out0 = n_in + n_riders
    rider_out = refs[out0 + n_out:out0 + n_out + n_riders]

    def ride():
        for src, dst in zip(rider_in, rider_out):
            dst[...] = src[...].astype(BF16)

    parts = (refs[:n_in], refs[out0:out0 + n_out], ride)
    if n_scratch:
        parts += (refs[len(refs) - n_scratch:],)
    return parts


def _prescale_kernel(x_ref, g_ref, xg_ref, ss_ref):
    x = x_ref[...]
    xg_ref[...] = (x * g_ref[...]).astype(BF16)
    ss_ref[...] = _lane_partial_sumsq(x)


def _prescale(x, g, layer, *, tm=512):
    s, d = x.shape
    return pl.pallas_call(
        _prescale_kernel,
        grid=(s // tm,),
        in_specs=[pl.BlockSpec((tm, d), lambda i: (i, 0)),
                  _layer_vec_spec(d, layer, 1)],
        out_specs=[pl.BlockSpec((tm, d), lambda i: (i, 0)),
                   pl.BlockSpec((tm, LANES), lambda i: (i, 0))],
        out_shape=[jax.ShapeDtypeStruct((s, d), BF16),
                   jax.ShapeDtypeStruct((s, LANES), F32)],
        compiler_params=_params("parallel"),
        name="prescale",
    )(x, g)


def _final_norm_kernel(x_ref, g_ref, o_ref):
    x = x_ref[...]
    ms = jnp.mean(x * x, axis=-1, keepdims=True)
    o_ref[...] = (x * lax.rsqrt(ms + EPS)) * g_ref[...]


def _final_norm(x, g, *, tm=512):
    s, d = x.shape
    return pl.pallas_call(
        _final_norm_kernel,
        grid=(s // tm,),
        in_specs=[pl.BlockSpec((tm, d), lambda i: (i, 0)),
                  pl.BlockSpec((1, d), lambda i: (0, 0))],
        out_specs=pl.BlockSpec((tm, d), lambda i: (i, 0)),
        out_shape=jax.ShapeDtypeStruct((s, d), F32),
        compiler_params=_params("parallel"),
        name="final_norm",
    )(x, g)


def _inproj_ag_kernel(*refs, n_riders, gelu_tiles):
    (xg_ref, ss_ref, w_ref), (o_ref,), ride = _split_refs(refs, 3, 1, n_riders)
    j = pl.program_id(1)

    def project(epilogue):
        r = _row_rms_scale(ss_ref)
        for r0 in range(0, o_ref.shape[0], EPILOGUE_ROWS):
            rows = slice(r0, r0 + EPILOGUE_ROWS)
            o_ref[rows, :] = epilogue(
                r[rows] * jnp.dot(xg_ref[rows, :], w_ref[...], preferred_element_type=F32))
        ride()

    @pl.when(j < gelu_tiles)
    def _():
        project(jax.nn.gelu)

    @pl.when(j >= gelu_tiles)
    def _():
        project(jax.nn.sigmoid)


def _inproj_ag(xg, ss, w, riders=(), *, tm=1024, tn=1024):
    s, d = xg.shape
    gelu_tiles = (2 * D_HALF) // tn
    skip = GATE_COL0 // tn - gelu_tiles
    n = 2 * D_HALF + 3 * D_MODEL
    nj = n // tn
    r_in, r_out, r_shapes = _rider_specs(riders, s // tm, nj)
    outs = pl.pallas_call(
        functools.partial(_inproj_ag_kernel, n_riders=len(riders), gelu_tiles=gelu_tiles),
        grid=(s // tm, nj),
        in_specs=[pl.BlockSpec((tm, d), lambda i, j: (i, 0)),
                  pl.BlockSpec((tm, LANES), lambda i, j: (i, 0)),
                  pl.BlockSpec((d, tn),
                               lambda i, j: (0, jnp.where(j < gelu_tiles, j, j + skip)))] + r_in,
        out_specs=[pl.BlockSpec((tm, tn), lambda i, j: (i, j))] + r_out,
        out_shape=[jax.ShapeDtypeStruct((s, n), F32)] + r_shapes,
        compiler_params=_params("arbitrary", "arbitrary"),
        name="inproj_ag",
    )(xg, ss, w, *(r.stack for r in riders))
    return outs[0], outs[1:]


def _inproj_b_kernel(*refs, n_riders, chunk):
    (xg_ref, ss_ref, w_ref, wp_ref, scale_ref), (y_ref,), ride, (halo_ref,) = _split_refs(
        refs, 5, 1, n_riders, 1)
    i = pl.program_id(0)
    g = pl.program_id(1)
    tm = y_ref.shape[0]

    def pool_group(k, win):
        r = _row_rms_scale(ss_ref)
        prev = jnp.where(i > 0, halo_ref[k], 0.0)
        for r0 in range(0, tm, chunk):
            rows = slice(r0, r0 + chunk)
            x = r[rows] * jnp.dot(xg_ref[rows, :], w_ref[...], preferred_element_type=F32)
            acc = jnp.concatenate([prev, x], axis=0)
            span = 1
            while span < win:
                acc = acc[span:, :] + acc[:-span, :]
                span *= 2
            wsum = acc[POOL_HALO - (win - 1):, :]
            pos = i * tm + r0 + lax.broadcasted_iota(jnp.int32, (chunk, 1), 0) + 1
            count = jnp.minimum(pos, win).astype(F32)
            pooled = (wsum / count - x).astype(BF16)
            y = jnp.dot(pooled, wp_ref[...], preferred_element_type=F32)
            y_ref[rows, :] = (y * scale_ref[...]).astype(BF16)
            prev = x[chunk - POOL_HALO:, :]
        halo_ref[k] = prev
        ride()

    for k, win in enumerate(POOL_WINDOWS):
        pl.when(g == k)(functools.partial(pool_group, k, win))


def _inproj_b(xg, ss, w, w_pool, scale, layer, riders=(), *, tm=1024):
    s, d = xg.shape
    ng = len(POOL_WINDOWS)
    gw = D_HALF // ng
    col0 = (2 * D_HALF) // gw
    r_in, r_out, r_shapes = _rider_specs(riders, s // tm, ng)
    outs = pl.pallas_call(
        functools.partial(_inproj_b_kernel, n_riders=len(riders), chunk=MIXER_ROWS),
        grid=(s // tm, ng),
        in_specs=[pl.BlockSpec((tm, d), lambda i, g: (i, 0)),
                  pl.BlockSpec((tm, LANES), lambda i, g: (i, 0)),
                  pl.BlockSpec((d, gw), lambda i, g: (0, col0 + g)),
                  pl.BlockSpec((None, None, gw, gw), lambda i, g: (layer, g, 0, 0)),
                  pl.BlockSpec((None, 1, gw), lambda i, g: (layer, 0, g))] + r_in,
        out_specs=[pl.BlockSpec((tm, gw), lambda i, g: (i, g))] + r_out,
        out_shape=[jax.ShapeDtypeStruct((s, D_HALF), BF16)] + r_shapes,
        scratch_shapes=[pltpu.VMEM((ng, POOL_HALO, gw), F32)],
        compiler_params=_params("arbitrary", "arbitrary"),
        name="inproj_b",
    )(xg, ss, w, w_pool, scale, *(r.stack for r in riders))
    return outs[0], outs[1:]


def _inproj_c_kernel(xg_ref, ss_ref, wb_ref, wc_ref, wh_ref, cw_ref, y_ref, halo_ref,
                     *, chunk):
    i = pl.program_id(0)
    c = pl.program_id(1)
    tm = y_ref.shape[0]
    r = _row_rms_scale(ss_ref)
    prev = jnp.where(i > 0, halo_ref[c], 0.0)
    for r0 in range(0, tm, chunk):
        rows = slice(r0, r0 + chunk)
        xg = xg_ref[rows, :]
        b_gate = r[rows] * jnp.dot(xg, wb_ref[...], preferred_element_type=F32)
        c_gate = r[rows] * jnp.dot(xg, wc_ref[...], preferred_element_type=F32)
        h_c = r[rows] * jnp.dot(xg, wh_ref[...], preferred_element_type=F32)
        z = c_gate * h_c
        ext = jnp.concatenate([prev, z], axis=0)
        conv = (cw_ref[0:1, :] * ext[CONV_HALO - 2:CONV_HALO - 2 + chunk, :]
                + cw_ref[1:2, :] * ext[CONV_HALO - 1:CONV_HALO - 1 + chunk, :]
                + cw_ref[2:3, :] * z)
        y_ref[rows, :] = (b_gate * conv).astype(BF16)
        prev = z[chunk - CONV_HALO:, :]
    halo_ref[c] = prev


def _inproj_c(xg, ss, w, conv_w, layer, *, tm=1024, tc=512):
    s, d = xg.shape
    nc = D_HALF // tc
    first = (3 * D_HALF) // tc

    def w_spec(section):
        return pl.BlockSpec((d, tc), lambda i, c: (0, first + section * nc + c))

    return pl.pallas_call(
        functools.partial(_inproj_c_kernel, chunk=MIXER_ROWS),
        grid=(s // tm, nc),
        in_specs=[pl.BlockSpec((tm, d), lambda i, c: (i, 0)),
                  pl.BlockSpec((tm, LANES), lambda i, c: (i, 0)),
                  w_spec(0), w_spec(1), w_spec(2),
                  pl.BlockSpec((None, 3, tc), lambda i, c: (layer, 0, c))],
        out_specs=pl.BlockSpec((tm, tc), lambda i, c: (i, c)),
        out_shape=jax.ShapeDtypeStruct((s, D_HALF), BF16),
        scratch_shapes=[pltpu.VMEM((nc, CONV_HALO, tc), F32)],
        compiler_params=_params("arbitrary", "arbitrary"),
        name="inproj_c",
    )(xg, ss, w, w, w, conv_w)


def _branch_a_kernel(u_ref, v_ref, lng_ref, lnb_ref, ws_ref, bias_ref, y_ref):
    tm = u_ref.shape[0]
    gw = D_HALF // A_GROUPS
    v = v_ref[...]
    mu = jnp.mean(v, axis=-1, keepdims=True)
    dv = v - mu
    var = jnp.mean(dv * dv, axis=-1, keepdims=True)
    vn = ((dv * lax.rsqrt(var + EPS)) * lng_ref[...] + lnb_ref[...]).astype(BF16)
    t_idx = lax.broadcasted_iota(jnp.int32, (A_CHUNK, A_CHUNK), 0)
    s_idx = lax.broadcasted_iota(jnp.int32, (A_CHUNK, A_CHUNK), 1)
    causal = s_idx <= t_idx
    for g in range(A_GROUPS):
        wg = jnp.where(causal, ws_ref[g], 0.0).astype(BF16)
        cols = slice(g * gw, (g + 1) * gw)
        for n in range(tm // A_CHUNK):
            rows = slice(n * A_CHUNK, (n + 1) * A_CHUNK)
            mixed = jnp.dot(wg, vn[rows, cols], preferred_element_type=F32)
            mixed = mixed + bias_ref[:, cols]
            y_ref[rows, cols] = (u_ref[rows, cols] * mixed).astype(BF16)


def _branch_a(proj, ln_g, ln_b, w_s, bias, layer, *, tm=512):
    s = proj.shape[0]
    return pl.pallas_call(
        _branch_a_kernel,
        grid=(s // tm,),
        in_specs=[pl.BlockSpec((tm, D_HALF), lambda i: (i, 0)),
                  pl.BlockSpec((tm, D_HALF), lambda i: (i, 1)),
                  _layer_vec_spec(D_HALF, layer, 1),
                  _layer_vec_spec(D_HALF, layer, 1),
                  pl.BlockSpec((None, A_GROUPS, A_CHUNK, A_CHUNK),
                               lambda i: (layer, 0, 0, 0)),
                  pl.BlockSpec((A_CHUNK, D_HALF), lambda i: (0, 0))],
        out_specs=pl.BlockSpec((tm, D_HALF), lambda i: (i, 0)),
        out_shape=jax.ShapeDtypeStruct((s, D_HALF), BF16),
        compiler_params=_params("parallel"),
        name="branch_a",
    )(proj, proj, ln_g, ln_b, w_s, bias)


def _merge_kernel(ya_ref, yb_ref, yc_ref, wa_ref, wb_ref, wc_ref,
                  ga_ref, gb_ref, gc_ref, o_ref):
    m = ga_ref[...] * jnp.dot(ya_ref[...], wa_ref[...], preferred_element_type=F32)
    m = m + gb_ref[...] * jnp.dot(yb_ref[...], wb_ref[...], preferred_element_type=F32)
    m = m + gc_ref[...] * jnp.dot(yc_ref[...], wc_ref[...], preferred_element_type=F32)
    o_ref[...] = m.astype(BF16)


def _merge(ya, yb, yc, wa, wb, wc, proj, layer, *, tm=1024, tn=512):
    s = ya.shape[0]
    g0 = (2 * D_HALF) // tn
    gstep = D_MODEL // tn
    y_spec = pl.BlockSpec((tm, D_HALF), lambda i, j: (i, 0))
    w_spec = _weight_cols_spec(wa, layer, tn)

    def gate_spec(k):
        return pl.BlockSpec((tm, tn), lambda i, j: (i, g0 + k * gstep + j))

    return pl.pallas_call(
        _merge_kernel,
        grid=(s // tm, D_MODEL // tn),
        in_specs=[y_spec, y_spec, y_spec, w_spec, w_spec, w_spec,
                  gate_spec(0), gate_spec(1), gate_spec(2)],
        out_specs=pl.BlockSpec((tm, tn), lambda i, j: (i, j)),
        out_shape=jax.ShapeDtypeStruct((s, D_MODEL), BF16),
        compiler_params=_params("parallel", "arbitrary"),
        name="merge",
    )(ya, yb, yc, wa, wb, wc, proj, proj, proj)


def _matmul_residual_kernel(*refs, n_riders, emit_norm_operand):
    if emit_norm_operand:
        (a_ref, w_ref, x_ref, g_ref), (o_ref, xg_ref, ss_ref), ride = _split_refs(
            refs, 4, 3, n_riders)
    else:
        (a_ref, w_ref, x_ref), (o_ref,), ride = _split_refs(refs, 3, 1, n_riders)
    if emit_norm_operand:
        @pl.when(pl.program_id(1) == 0)
        def _():
            ss_ref[...] = jnp.zeros_like(ss_ref)

    for r0 in range(0, o_ref.shape[0], EPILOGUE_ROWS):
        rows = slice(r0, r0 + EPILOGUE_ROWS)
        y = x_ref[rows, :] + jnp.dot(a_ref[rows, :], w_ref[...], preferred_element_type=F32)
        o_ref[rows, :] = y
        if emit_norm_operand:
            xg_ref[rows, :] = (y * g_ref[...]).astype(BF16)
            ss_ref[rows, :] += _lane_partial_sumsq(y)
    ride()


def _matmul_residual(a, w, x, layer, norm_g=None, norm_layer=None, riders=(), *, tm, tn, name):
    s, k = a.shape
    n = w.shape[-1]
    nj = n // tn
    emit = norm_g is not None
    tile = pl.BlockSpec((tm, tn), lambda i, j: (i, j))
    in_specs = [pl.BlockSpec((tm, k), lambda i, j: (i, 0)), _weight_cols_spec(w, layer, tn), tile]
    out_specs = [tile]
    out_shape = [jax.ShapeDtypeStruct((s, n), F32)]
    args = [a, w, x]
    if emit:
        in_specs.append(pl.BlockSpec((None, 1, tn), lambda i, j: (norm_layer, 0, j)))
        args.append(norm_g)
        out_specs += [tile, pl.BlockSpec((tm, LANES), lambda i, j: (i, 0))]
        out_shape += [jax.ShapeDtypeStruct((s, n), BF16), jax.ShapeDtypeStruct((s, LANES), F32)]
    r_in, r_out, r_shapes = _rider_specs(riders, s // tm, nj)
    outs = pl.pallas_call(
        functools.partial(_matmul_residual_kernel, n_riders=len(riders), emit_norm_operand=emit),
        grid=(s // tm, nj),
        in_specs=in_specs + r_in,
        out_specs=out_specs + r_out,
        out_shape=out_shape + r_shapes,
        compiler_params=_params("arbitrary", "arbitrary"),
        name=name,
    )(*args, *(r.stack for r in riders))
    n_main = len(out_shape)
    return outs[:n_main], outs[n_main:]


def _ffn_up_kernel(*refs, n_riders):
    (xg_ref, ss_ref, wg_ref, wu_ref), (o_ref,), ride = _split_refs(refs, 4, 1, n_riders)
    r = _row_rms_scale(ss_ref)
    for r0 in range(0, o_ref.shape[0], EPILOGUE_ROWS):
        rows = slice(r0, r0 + EPILOGUE_ROWS)
        xg = xg_ref[rows, :]
        gate = r[rows] * jnp.dot(xg, wg_ref[...], preferred_element_type=F32)
        up = r[rows] * jnp.dot(xg, wu_ref[...], preferred_element_type=F32)
        o_ref[rows, :] = (jax.nn.silu(gate) * up).astype(BF16)
    ride()


def _ffn_up(xg, ss, wg, wu, layer, riders=(), *, tm=2048, tn=256):
    s, d = xg.shape
    f = wg.shape[-1]
    nj = f // tn
    r_in, r_out, r_shapes = _rider_specs(riders, s // tm, nj)
    outs = pl.pallas_call(
        functools.partial(_ffn_up_kernel, n_riders=len(riders)),
        grid=(s // tm, nj),
        in_specs=[pl.BlockSpec((tm, d), lambda i, j: (i, 0)),
                  pl.BlockSpec((tm, LANES), lambda i, j: (i, 0)),
                  _weight_cols_spec(wg, layer, tn),
                  _weight_cols_spec(wu, layer, tn)] + r_in,
        out_specs=[pl.BlockSpec((tm, tn), lambda i, j: (i, j))] + r_out,
        out_shape=[jax.ShapeDtypeStruct((s, f), BF16)] + r_shapes,
        compiler_params=_params("arbitrary", "arbitrary"),
        name="ffn_up",
    )(xg, ss, wg, wu, *(r.stack for r in riders))
    return outs[0], outs[1:]


def kernel(x, norm_mix_g, w_in, ln_a_g, ln_a_b, w_spatial, b_spatial, w_pool,
           pool_scale, conv_w, w_branch_a, w_branch_b, w_branch_c, w_out,
           norm_ffn_g, w_ffn_gate, w_ffn_up, w_ffn_down, final_norm_g):
    bsz, s, d = x.shape
    depth = w_in.shape[0]
    assert (bsz, d) == (1, D_MODEL) and w_in.shape[2] == N_IN
    xs = x.reshape(s, d)
    gw_a = D_HALF // A_GROUPS
    w_pool, w_branch_a, w_branch_b, w_branch_c, w_out = (
        w.astype(BF16) for w in (w_pool, w_branch_a, w_branch_b, w_branch_c, w_out))
    norm_mix_g, ln_a_g, ln_a_b, pool_scale, norm_ffn_g = (
        v[:, None, :] for v in (norm_mix_g, ln_a_g, ln_a_b, pool_scale, norm_ffn_g))

    def ffn_riders(layer):
        return (CastRider(w_ffn_gate, layer), CastRider(w_ffn_up, layer),
                CastRider(w_ffn_down, layer))

    w_in_l = w_in[0].astype(BF16)
    xg, ss = _prescale(xs, norm_mix_g, 0)
    ffn_w = None
    for l in range(depth):
        last = l + 1 == depth
        if l == 0:
            wg_rider, wu_rider, wd_rider = ffn_riders(0)
            proj, (wg, wu) = _inproj_ag(xg, ss, w_in_l, (wg_rider, wu_rider))
            yb, (wd,) = _inproj_b(xg, ss, w_in_l, w_pool, pool_scale, l, (wd_rider,))
            ffn_w = (wg, wu, wd)
        else:
            proj, _ = _inproj_ag(xg, ss, w_in_l)
            yb, _ = _inproj_b(xg, ss, w_in_l, w_pool, pool_scale, l)
        yc = _inproj_c(xg, ss, w_in_l, conv_w, l)
        bias = jnp.repeat(b_spatial[l].T, gw_a, axis=1)
        ya = _branch_a(proj, ln_a_g, ln_a_b, w_spatial, bias, l)
        merged = _merge(ya, yb, yc, w_branch_a, w_branch_b, w_branch_c, proj, l)
        (xs, xg, ss), _ = _matmul_residual(merged, w_out, xs, l, norm_ffn_g, l,
                                           tm=1024, tn=512, name="outproj")
        wg, wu, wd = ffn_w
        ff, ffn_w = _ffn_up(xg, ss, wg, wu, l, () if last else ffn_riders(l + 1))
        if last:
            (xs,), _ = _matmul_residual(ff, wd, xs, l, tm=512, tn=512, name="ffn_down")
        else:
            (xs, xg, ss), (w_in_l,) = _matmul_residual(
                ff, wd, xs, l, norm_mix_g, l + 1, (CastRider(w_in, l + 1),),
                tm=512, tn=512, name="ffn_down")
    return _final_norm(xs, final_norm_g[None, :]).reshape(bsz, s, d)
```

```python
import functools
from typing import NamedTuple

import jax
import jax.numpy as jnp
from jax import lax
from jax.experimental import pallas as pl
from jax.experimental.pallas import tpu as pltpu

D_MODEL = 4096
D_HALF = D_MODEL // 2
A_GROUPS = 8
A_CHUNK = 128
POOL_WINDOWS = (2, 4, 8, 16)
POOL_HALO = 16
CONV_HALO = 8
N_IN = 12 * D_HALF
GATE_COL0 = 6 * D_HALF
EPS = 1e-6
EPILOGUE_ROWS = 256
MIXER_ROWS = 256
LANES = 128
BF16_SUBLANE_TILE = 16

V7X_VMEM_LIMIT_BYTES = 56 * 1024 * 1024

F32 = jnp.float32
BF16 = jnp.bfloat16


def _params(*semantics):
    return pltpu.CompilerParams(dimension_semantics=semantics,
                                vmem_limit_bytes=V7X_VMEM_LIMIT_BYTES)


def _layer_vec_spec(width, layer, grid_rank):
    if grid_rank == 1:
        return pl.BlockSpec((None, 1, width), lambda i: (layer, 0, 0))
    return pl.BlockSpec((None, 1, width), lambda i, j: (layer, 0, 0))


def _weight_cols_spec(w, layer, tn):
    if w.ndim == 2:
        return pl.BlockSpec((w.shape[0], tn), lambda i, j: (0, j))
    return pl.BlockSpec((None, w.shape[1], tn), lambda i, j: (layer, 0, j))


def _lane_partial_sumsq(x):
    sq = x * x
    acc = sq[:, :LANES]
    for c in range(LANES, x.shape[1], LANES):
        acc = acc + sq[:, c:c + LANES]
    return acc


def _row_rms_scale(ss_ref):
    ms = jnp.sum(ss_ref[...], axis=-1, keepdims=True) * (1.0 / D_MODEL)
    return lax.rsqrt(ms + EPS)


class CastRider(NamedTuple):
    stack: jax.Array
    layer: int


def _rider_block_rows(rows, n_steps):
    for rb in range(BF16_SUBLANE_TILE, rows + 1, BF16_SUBLANE_TILE):
        if rows % rb == 0 and rows // rb <= n_steps:
            return rb
    raise ValueError(f"cannot convert {rows} rows in {n_steps} grid steps")


def _rider_specs(riders, n_outer, n_inner):
    in_specs, out_specs, out_shapes = [], [], []
    for stack, layer in riders:
        _, rows, cols = stack.shape
        rb = _rider_block_rows(rows, n_outer * n_inner)
        last = rows // rb - 1

        def block(i, j, last=last):
            return jnp.minimum(i * n_inner + j, last)

        in_specs.append(pl.BlockSpec((None, rb, cols),
                                     lambda i, j, layer=layer, block=block: (layer, block(i, j), 0)))
        out_specs.append(pl.BlockSpec((rb, cols), lambda i, j, block=block: (block(i, j), 0)))
        out_shapes.append(jax.ShapeDtypeStruct((rows, cols), BF16))
    return in_specs, out_specs, out_shapes


def _split_refs(refs, n_in, n_out, n_riders, n_scratch=0):
    rider_in = refs[n_in:n_in + n_riders]
    out0 = n_in + n_riders
    rider_out = refs[out0 + n_out:out0 + n_out + n_riders]

    def ride():
        for src, dst in zip(rider_in, rider_out):
            dst[...] = src[...].astype(BF16)

    parts = (refs[:n_in], refs[out0:out0 + n_out], ride)
    if n_scratch:
        parts += (refs[len(refs) - n_scratch:],)
    return parts


def _prescale_kernel(x_ref, g_ref, xg_ref, ss_ref):
    x = x_ref[...]
    xg_ref[...] = (x * g_ref[...]).astype(BF16)
    ss_ref[...] = _lane_partial_sumsq(x)


def _prescale(x, g, layer, *, tm=512):
    s, d = x.shape
    return pl.pallas_call(
        _prescale_kernel,
        grid=(s // tm,),
        in_specs=[pl.BlockSpec((tm, d), lambda i: (i, 0)),
                  _layer_vec_spec(d, layer, 1)],
        out_specs=[pl.BlockSpec((tm, d), lambda i: (i, 0)),
                   pl.BlockSpec((tm, LANES), lambda i: (i, 0))],
        out_shape=[jax.ShapeDtypeStruct((s, d), BF16),
                   jax.ShapeDtypeStruct((s, LANES), F32)],
        compiler_params=_params("parallel"),
        name="prescale",
    )(x, g)


def _final_norm_kernel(x_ref, g_ref, o_ref):
    x = x_ref[...]
    ms = jnp.mean(x * x, axis=-1, keepdims=True)
    o_ref[...] = (x * lax.rsqrt(ms + EPS)) * g_ref[...]


def _final_norm(x, g, *, tm=512):
    s, d = x.shape
    return pl.pallas_call(
        _final_norm_kernel,
        grid=(s // tm,),
        in_specs=[pl.BlockSpec((tm, d), lambda i: (i, 0)),
                  pl.BlockSpec((1, d), lambda i: (0, 0))],
        out_specs=pl.BlockSpec((tm, d), lambda i: (i, 0)),
        out_shape=jax.ShapeDtypeStruct((s, d), F32),
        compiler_params=_params("parallel"),
        name="final_norm",
    )(x, g)


def _inproj_ag_kernel(*refs, n_riders, gelu_tiles):
    (xg_ref, ss_ref, w_ref), (o_ref,), ride = _split_refs(refs, 3, 1, n_riders)
    j = pl.program_id(1)

    def project(epilogue):
        r = _row_rms_scale(ss_ref)
        for r0 in range(0, o_ref.shape[0], EPILOGUE_ROWS):
            rows = slice(r0, r0 + EPILOGUE_ROWS)
            o_ref[rows, :] = epilogue(
                r[rows] * jnp.dot(xg_ref[rows, :], w_ref[...], preferred_element_type=F32))
        ride()

    @pl.when(j < gelu_tiles)
    def _():
        project(jax.nn.gelu)

    @pl.when(j >= gelu_tiles)
    def _():
        project(jax.nn.sigmoid)


def _inproj_ag(xg, ss, w, riders=(), *, tm=2048, tn=512):
    s, d = xg.shape
    gelu_tiles = (2 * D_HALF) // tn
    skip = GATE_COL0 // tn - gelu_tiles
    n = 2 * D_HALF + 3 * D_MODEL
    nj = n // tn
    r_in, r_out, r_shapes = _rider_specs(riders, s // tm, nj)
    outs = pl.pallas_call(
        functools.partial(_inproj_ag_kernel, n_riders=len(riders), gelu_tiles=gelu_tiles),
        grid=(s // tm, nj),
        in_specs=[pl.BlockSpec((tm, d), lambda i, j: (i, 0)),
                  pl.BlockSpec((tm, LANES), lambda i, j: (i, 0)),
                  pl.BlockSpec((d, tn),
                               lambda i, j: (0, jnp.where(j < gelu_tiles, j, j + skip)))] + r_in,
        out_specs=[pl.BlockSpec((tm, tn), lambda i, j: (i, j))] + r_out,
        out_shape=[jax.ShapeDtypeStruct((s, n), F32)] + r_shapes,
        compiler_params=_params("arbitrary", "arbitrary"),
        name="inproj_ag",
    )(xg, ss, w, *(r.stack for r in riders))
    return outs[0], outs[1:]


def _inproj_b_kernel(*refs, n_riders, chunk):
    (xg_ref, ss_ref, w_ref, wp_ref, scale_ref), (y_ref,), ride, (halo_ref,) = _split_refs(
        refs, 5, 1, n_riders, 1)
    i = pl.program_id(0)
    g = pl.program_id(1)
    tm = y_ref.shape[0]

    def pool_group(k, win):
        r = _row_rms_scale(ss_ref)
        prev = jnp.where(i > 0, halo_ref[k], 0.0)
        for r0 in range(0, tm, chunk):
            rows = slice(r0, r0 + chunk)
            x = r[rows] * jnp.dot(xg_ref[rows, :], w_ref[...], preferred_element_type=F32)
            acc = jnp.concatenate([prev, x], axis=0)
            span = 1
            while span < win:
                acc = acc[span:, :] + acc[:-span, :]
                span *= 2
            wsum = acc[POOL_HALO - (win - 1):, :]
            pos = i * tm + r0 + lax.broadcasted_iota(jnp.int32, (chunk, 1), 0) + 1
            count = jnp.minimum(pos, win).astype(F32)
            pooled = (wsum / count - x).astype(BF16)
            y = jnp.dot(pooled, wp_ref[...], preferred_element_type=F32)
            y_ref[rows, :] = (y * scale_ref[...]).astype(BF16)
            prev = x[chunk - POOL_HALO:, :]
        halo_ref[k] = prev
        ride()

    for k, win in enumerate(POOL_WINDOWS):
        pl.when(g == k)(functools.partial(pool_group, k, win))


def _inproj_b(xg, ss, w, w_pool, scale, layer, riders=(), *, tm=1024):
    s, d = xg.shape
    ng = len(POOL_WINDOWS)
    gw = D_HALF // ng
    col0 = (2 * D_HALF) // gw
    r_in, r_out, r_shapes = _rider_specs(riders, s // tm, ng)
    outs = pl.pallas_call(
        functools.partial(_inproj_b_kernel, n_riders=len(riders), chunk=MIXER_ROWS),
        grid=(s // tm, ng),
        in_specs=[pl.BlockSpec((tm, d), lambda i, g: (i, 0)),
                  pl.BlockSpec((tm, LANES), lambda i, g: (i, 0)),
                  pl.BlockSpec((d, gw), lambda i, g: (0, col0 + g)),
                  pl.BlockSpec((None, None, gw, gw), lambda i, g: (layer, g, 0, 0)),
                  pl.BlockSpec((None, 1, gw), lambda i, g: (layer, 0, g))] + r_in,
        out_specs=[pl.BlockSpec((tm, gw), lambda i, g: (i, g))] + r_out,
        out_shape=[jax.ShapeDtypeStruct((s, D_HALF), BF16)] + r_shapes,
        scratch_shapes=[pltpu.VMEM((ng, POOL_HALO, gw), F32)],
        compiler_params=_params("arbitrary", "arbitrary"),
        name="inproj_b",
    )(xg, ss, w, w_pool, scale, *(r.stack for r in riders))
    return outs[0], outs[1:]


def _inproj_c_kernel(xg_ref, ss_ref, wb_ref, wc_ref, wh_ref, cw_ref, y_ref, halo_ref,
                     *, chunk):
    i = pl.program_id(0)
    c = pl.program_id(1)
    tm = y_ref.shape[0]
    r = _row_rms_scale(ss_ref)
    prev = jnp.where(i > 0, halo_ref[c], 0.0)
    for r0 in range(0, tm, chunk):
        rows = slice(r0, r0 + chunk)
        xg = xg_ref[rows, :]
        b_gate = r[rows] * jnp.dot(xg, wb_ref[...], preferred_element_type=F32)
        c_gate = r[rows] * jnp.dot(xg, wc_ref[...], preferred_element_type=F32)
        h_c = r[rows] * jnp.dot(xg, wh_ref[...], preferred_element_type=F32)
        z = c_gate * h_c
        ext = jnp.concatenate([prev, z], axis=0)
        conv = (cw_ref[0:1, :] * ext[CONV_HALO - 2:CONV_HALO - 2 + chunk, :]
                + cw_ref[1:2, :] * ext[CONV_HALO - 1:CONV_HALO - 1 + chunk, :]
                + cw_ref[2:3, :] * z)
        y_ref[rows, :] = (b_gate * conv).astype(BF16)
        prev = z[chunk - CONV_HALO:, :]
    halo_ref[c] = prev


def _inproj_c(xg, ss, w, conv_w, layer, *, tm=2048, tc=256):
    s, d = xg.shape
    nc = D_HALF // tc
    first = (3 * D_HALF) // tc

    def w_spec(section):
        return pl.BlockSpec((d, tc), lambda i, c: (0, first + section * nc + c))

    return pl.pallas_call(
        functools.partial(_inproj_c_kernel, chunk=MIXER_ROWS),
        grid=(s // tm, nc),
        in_specs=[pl.BlockSpec((tm, d), lambda i, c: (i, 0)),
                  pl.BlockSpec((tm, LANES), lambda i, c: (i, 0)),
                  w_spec(0), w_spec(1), w_spec(2),
                  pl.BlockSpec((None, 3, tc), lambda i, c: (layer, 0, c))],
        out_specs=pl.BlockSpec((tm, tc), lambda i, c: (i, c)),
        out_shape=jax.ShapeDtypeStruct((s, D_HALF), BF16),
        scratch_shapes=[pltpu.VMEM((nc, CONV_HALO, tc), F32)],
        compiler_params=_params("arbitrary", "arbitrary"),
        name="inproj_c",
    )(xg, ss, w, w, w, conv_w)


def _branch_a_kernel(u_ref, v_ref, lng_ref, lnb_ref, ws_ref, bias_ref, y_ref):
    tm = u_ref.shape[0]
    gw = D_HALF // A_GROUPS
    v = v_ref[...]
    mu = jnp.mean(v, axis=-1, keepdims=True)
    dv = v - mu
    var = jnp.mean(dv * dv, axis=-1, keepdims=True)
    vn = ((dv * lax.rsqrt(var + EPS)) * lng_ref[...] + lnb_ref[...]).astype(BF16)
    t_idx = lax.broadcasted_iota(jnp.int32, (A_CHUNK, A_CHUNK), 0)
    s_idx = lax.broadcasted_iota(jnp.int32, (A_CHUNK, A_CHUNK), 1)
    causal = s_idx <= t_idx
    for g in range(A_GROUPS):
        wg = jnp.where(causal, ws_ref[g], 0.0).astype(BF16)
        cols = slice(g * gw, (g + 1) * gw)
        for n in range(tm // A_CHUNK):
            rows = slice(n * A_CHUNK, (n + 1) * A_CHUNK)
            mixed = jnp.dot(wg, vn[rows, cols], preferred_element_type=F32)
            mixed = mixed + bias_ref[:, cols]
            y_ref[rows, cols] = (u_ref[rows, cols] * mixed).astype(BF16)


def _branch_a(proj, ln_g, ln_b, w_s, bias, layer, *, tm=512):
    s = proj.shape[0]
    return pl.pallas_call(
        _branch_a_kernel,
        grid=(s // tm,),
        in_specs=[pl.BlockSpec((tm, D_HALF), lambda i: (i, 0)),
                  pl.BlockSpec((tm, D_HALF), lambda i: (i, 1)),
                  _layer_vec_spec(D_HALF, layer, 1),
                  _layer_vec_spec(D_HALF, layer, 1),
                  pl.BlockSpec((None, A_GROUPS, A_CHUNK, A_CHUNK),
                               lambda i: (layer, 0, 0, 0)),
                  pl.BlockSpec((A_CHUNK, D_HALF), lambda i: (0, 0))],
        out_specs=pl.BlockSpec((tm, D_HALF), lambda i: (i, 0)),
        out_shape=jax.ShapeDtypeStruct((s, D_HALF), BF16),
        compiler_params=_params("parallel"),
        name="branch_a",
    )(proj, proj, ln_g, ln_b, w_s, bias)


def _merge_kernel(ya_ref, yb_ref, yc_ref, wa_ref, wb_ref, wc_ref,
                  ga_ref, gb_ref, gc_ref, o_ref):
    m = ga_ref[...] * jnp.dot(ya_ref[...], wa_ref[...], preferred_element_type=F32)
    m = m + gb_ref[...] * jnp.dot(yb_ref[...], wb_ref[...], preferred_element_type=F32)
    m = m + gc_ref[...] * jnp.dot(yc_ref[...], wc_ref[...], preferred_element_type=F32)
    o_ref[...] = m.astype(BF16)


def _merge(ya, yb, yc, wa, wb, wc, proj, layer, *, tm=1024, tn=512):
    s = ya.shape[0]
    g0 = (2 * D_HALF) // tn
    gstep = D_MODEL // tn
    y_spec = pl.BlockSpec((tm, D_HALF), lambda i, j: (i, 0))
    w_spec = _weight_cols_spec(wa, layer, tn)

    def gate_spec(k):
        return pl.BlockSpec((tm, tn), lambda i, j: (i, g0 + k * gstep + j))

    return pl.pallas_call(
        _merge_kernel,
        grid=(s // tm, D_MODEL // tn),
        in_specs=[y_spec, y_spec, y_spec, w_spec, w_spec, w_spec,
                  gate_spec(0), gate_spec(1), gate_spec(2)],
        out_specs=pl.BlockSpec((tm, tn), lambda i, j: (i, j)),
        out_shape=jax.ShapeDtypeStruct((s, D_MODEL), BF16),
        compiler_params=_params("parallel", "arbitrary"),
        name="merge",
    )(ya, yb, yc, wa, wb, wc, proj, proj, proj)


def _matmul_residual_kernel(*refs, n_riders, emit_norm_operand):
    if emit_norm_operand:
        (a_ref, w_ref, x_ref, g_ref), (o_ref, xg_ref, ss_ref), ride = _split_refs(
            refs, 4, 3, n_riders)
    else:
        (a_ref, w_ref, x_ref), (o_ref,), ride = _split_refs(refs, 3, 1, n_riders)
    if emit_norm_operand:
        @pl.when(pl.program_id(1) == 0)
        def _():
            ss_ref[...] = jnp.zeros_like(ss_ref)

    for r0 in range(0, o_ref.shape[0], EPILOGUE_ROWS):
        rows = slice(r0, r0 + EPILOGUE_ROWS)
        y = x_ref[rows, :] + jnp.dot(a_ref[rows, :], w_ref[...], preferred_element_type=F32)
        o_ref[rows, :] = y
        if emit_norm_operand:
            xg_ref[rows, :] = (y * g_ref[...]).astype(BF16)
            ss_ref[rows, :] += _lane_partial_sumsq(y)
    ride()


def _matmul_residual(a, w, x, layer, norm_g=None, norm_layer=None, riders=(), *, tm, tn, name):
    s, k = a.shape
    n = w.shape[-1]
    nj = n // tn
    emit = norm_g is not None
    tile = pl.BlockSpec((tm, tn), lambda i, j: (i, j))
    in_specs = [pl.BlockSpec((tm, k), lambda i, j: (i, 0)), _weight_cols_spec(w, layer, tn), tile]
    out_specs = [tile]
    out_shape = [jax.ShapeDtypeStruct((s, n), F32)]
    args = [a, w, x]
    if emit:
        in_specs.append(pl.BlockSpec((None, 1, tn), lambda i, j: (norm_layer, 0, j)))
        args.append(norm_g)
        out_specs += [tile, pl.BlockSpec((tm, LANES), lambda i, j: (i, 0))]
        out_shape += [jax.ShapeDtypeStruct((s, n), BF16), jax.ShapeDtypeStruct((s, LANES), F32)]
    r_in, r_out, r_shapes = _rider_specs(riders, s // tm, nj)
    outs = pl.pallas_call(
        functools.partial(_matmul_residual_kernel, n_riders=len(riders), emit_norm_operand=emit),
        grid=(s // tm, nj),
        in_specs=in_specs + r_in,
        out_specs=out_specs + r_out,
        out_shape=out_shape + r_shapes,
        compiler_params=_params("arbitrary", "arbitrary"),
        name=name,
    )(*args, *(r.stack for r in riders))
    n_main = len(out_shape)
    return outs[:n_main], outs[n_main:]


def _ffn_up_kernel(*refs, n_riders):
    (xg_ref, ss_ref, wg_ref, wu_ref), (o_ref,), ride = _split_refs(refs, 4, 1, n_riders)
    r = _row_rms_scale(ss_ref)
    for r0 in range(0, o_ref.shape[0], EPILOGUE_ROWS):
        rows = slice(r0, r0 + EPILOGUE_ROWS)
        xg = xg_ref[rows, :]
        gate = r[rows] * jnp.dot(xg, wg_ref[...], preferred_element_type=F32)
        up = r[rows] * jnp.dot(xg, wu_ref[...], preferred_element_type=F32)
        o_ref[rows, :] = (jax.nn.silu(gate) * up).astype(BF16)
    ride()


def _ffn_up(xg, ss, wg, wu, layer, riders=(), *, tm=2048, tn=256):
    s, d = xg.shape
    f = wg.shape[-1]
    nj = f // tn
    r_in, r_out, r_shapes = _rider_specs(riders, s // tm, nj)
    outs = pl.pallas_call(
        functools.partial(_ffn_up_kernel, n_riders=len(riders)),
        grid=(s // tm, nj),
        in_specs=[pl.BlockSpec((tm, d), lambda i, j: (i, 0)),
                  pl.BlockSpec((tm, LANES), lambda i, j: (i, 0)),
                  _weight_cols_spec(wg, layer, tn),
                  _weight_cols_spec(wu, layer, tn)] + r_in,
        out_specs=[pl.BlockSpec((tm, tn), lambda i, j: (i, j))] + r_out,
        out_shape=[jax.ShapeDtypeStruct((s, f), BF16)] + r_shapes,
        compiler_params=_params("arbitrary", "arbitrary"),
        name="ffn_up",
    )(xg, ss, wg, wu, *(r.stack for r in riders))
    return outs[0], outs[1:]


def kernel(x, norm_mix_g, w_in, ln_a_g, ln_a_b, w_spatial, b_spatial, w_pool,
           pool_scale, conv_w, w_branch_a, w_branch_b, w_branch_c, w_out,
           norm_ffn_g, w_ffn_gate, w_ffn_up, w_ffn_down, final_norm_g):
    bsz, s, d = x.shape
    depth = w_in.shape[0]
    assert (bsz, d) == (1, D_MODEL) and w_in.shape[2] == N_IN
    xs = x.reshape(s, d)
    gw_a = D_HALF // A_GROUPS
    w_pool, w_branch_a, w_branch_b, w_branch_c, w_out = (
        w.astype(BF16) for w in (w_pool, w_branch_a, w_branch_b, w_branch_c, w_out))
    norm_mix_g, ln_a_g, ln_a_b, pool_scale, norm_ffn_g = (
        v[:, None, :] for v in (norm_mix_g, ln_a_g, ln_a_b, pool_scale, norm_ffn_g))

    def ffn_riders(layer):
        return (CastRider(w_ffn_gate, layer), CastRider(w_ffn_up, layer),
                CastRider(w_ffn_down, layer))

    w_in_l = w_in[0].astype(BF16)
    xg, ss = _prescale(xs, norm_mix_g, 0)
    ffn_w = None
    for l in range(depth):
        last = l + 1 == depth
        if l == 0:
            wg_rider, wu_rider, wd_rider = ffn_riders(0)
            proj, (wg, wu) = _inproj_ag(xg, ss, w_in_l, (wg_rider, wu_rider))
            yb, (wd,) = _inproj_b(xg, ss, w_in_l, w_pool, pool_scale, l, (wd_rider,))
            ffn_w = (wg, wu, wd)
        else:
            proj, _ = _inproj_ag(xg, ss, w_in_l)
            yb, _ = _inproj_b(xg, ss, w_in_l, w_pool, pool_scale, l)
        yc = _inproj_c(xg, ss, w_in_l, conv_w, l)
        bias = jnp.repeat(b_spatial[l].T, gw_a, axis=1)
        ya = _branch_a(proj, ln_a_g, ln_a_b, w_spatial, bias, l)
        merged = _merge(ya, yb, yc, w_branch_a, w_branch_b, w_branch_c, proj, l)
        (xs, xg, ss), _ = _matmul_residual(merged, w_out, xs, l, norm_ffn_g, l,
                                           tm=2048, tn=256, name="outproj")
        wg, wu, wd = ffn_w
        ff, ffn_w = _ffn_up(xg, ss, wg, wu, l, () if last else ffn_riders(l + 1))
        if last:
            (xs,), _ = _matmul_residual(ff, wd, xs, l, tm=512, tn=512, name="ffn_down")
        else:
            (xs, xg, ss), (w_in_l,) = _matmul_residual(
                ff, wd, xs, l, norm_mix_g, l + 1, (CastRider(w_in, l + 1),),
                tm=512, tn=512, name="ffn_down")
    return _final_norm(xs, final_norm_g[None, :]).reshape(bsz, s, d)
```

```python
import functools
from typing import NamedTuple

import jax
import jax.numpy as jnp
from jax import lax
from jax.experimental import pallas as pl
from jax.experimental.pallas import tpu as pltpu

D_MODEL = 4096
D_HALF = D_MODEL // 2
A_GROUPS = 8
A_CHUNK = 128
POOL_WINDOWS = (2, 4, 8, 16)
POOL_HALO = 16
CONV_HALO = 8
N_IN = 12 * D_HALF
GATE_COL0 = 6 * D_HALF
EPS = 1e-6
EPILOGUE_ROWS = 256
CONV_MIXER_ROWS = 256
POOL_MIXER_ROWS = 512
LANES = 128
BF16_SUBLANE_TILE = 16

V7X_VMEM_LIMIT_BYTES = 56 * 1024 * 1024

F32 = jnp.float32
BF16 = jnp.bfloat16


def _params(*semantics):
    return pltpu.CompilerParams(dimension_semantics=semantics,
                                vmem_limit_bytes=V7X_VMEM_LIMIT_BYTES)


def _layer_vec_spec(width, layer, grid_rank):
    if grid_rank == 1:
        return pl.BlockSpec((None, 1, width), lambda i: (layer, 0, 0))
    return pl.BlockSpec((None, 1, width), lambda i, j: (layer, 0, 0))


def _weight_cols_spec(w, layer, tn):
    if w.ndim == 2:
        return pl.BlockSpec((w.shape[0], tn), lambda i, j: (0, j))
    return pl.BlockSpec((None, w.shape[1], tn), lambda i, j: (layer, 0, j))


def _lane_partial_sumsq(x):
    sq = x * x
    acc = sq[:, :LANES]
    for c in range(LANES, x.shape[1], LANES):
        acc = acc + sq[:, c:c + LANES]
    return acc


def _row_rms_scale(ss_ref):
    ms = jnp.sum(ss_ref[...], axis=-1, keepdims=True) * (1.0 / D_MODEL)
    return lax.rsqrt(ms + EPS)


class CastRider(NamedTuple):
    stack: jax.Array
    layer: int


def _rider_block_rows(rows, n_steps):
    for rb in range(BF16_SUBLANE_TILE, rows + 1, BF16_SUBLANE_TILE):
        if rows % rb == 0 and rows // rb <= n_steps:
            return rb
    raise ValueError(f"cannot convert {rows} rows in {n_steps} grid steps")


def _rider_specs(riders, n_outer, n_inner):
    in_specs, out_specs, out_shapes = [], [], []
    for stack, layer in riders:
        _, rows, cols = stack.shape
        rb = _rider_block_rows(rows, n_outer * n_inner)
        last = rows // rb - 1

        def block(i, j, last=last):
            return jnp.minimum(i * n_inner + j, last)

        in_specs.append(pl.BlockSpec((None, rb, cols),
                                     lambda i, j, layer=layer, block=block: (layer, block(i, j), 0)))
        out_specs.append(pl.BlockSpec((rb, cols), lambda i, j, block=block: (block(i, j), 0)))
        out_shapes.append(jax.ShapeDtypeStruct((rows, cols), BF16))
    return in_specs, out_specs, out_shapes


def _split_refs(refs, n_in, n_out, n_riders, n_scratch=0):
    rider_in = refs[n_in:n_in + n_riders]
    out0 = n_in + n_riders
    rider_out = refs[out0 + n_out:out0 + n_out + n_riders]

    def ride():
        for src, dst in zip(rider_in, rider_out):
            dst[...] = src[...].astype(BF16)

    parts = (refs[:n_in], refs[out0:out0 + n_out], ride)
    if n_scratch:
        parts += (refs[len(refs) - n_scratch:],)
    return parts


def _prescale_kernel(x_ref, g_ref, xg_ref, ss_ref):
    x = x_ref[...]
    xg_ref[...] = (x * g_ref[...]).astype(BF16)
    ss_ref[...] = _lane_partial_sumsq(x)


def _prescale(x, g, layer, *, tm=512):
    s, d = x.shape
    return pl.pallas_call(
        _prescale_kernel,
        grid=(s // tm,),
        in_specs=[pl.BlockSpec((tm, d), lambda i: (i, 0)),
                  _layer_vec_spec(d, layer, 1)],
        out_specs=[pl.BlockSpec((tm, d), lambda i: (i, 0)),
                   pl.BlockSpec((tm, LANES), lambda i: (i, 0))],
        out_shape=[jax.ShapeDtypeStruct((s, d), BF16),
                   jax.ShapeDtypeStruct((s, LANES), F32)],
        compiler_params=_params("parallel"),
        name="prescale",
    )(x, g)


def _final_norm_kernel(x_ref, g_ref, o_ref):
    x = x_ref[...]
    ms = jnp.mean(x * x, axis=-1, keepdims=True)
    o_ref[...] = (x * lax.rsqrt(ms + EPS)) * g_ref[...]


def _final_norm(x, g, *, tm=512):
    s, d = x.shape
    return pl.pallas_call(
        _final_norm_kernel,
        grid=(s // tm,),
        in_specs=[pl.BlockSpec((tm, d), lambda i: (i, 0)),
                  pl.BlockSpec((1, d), lambda i: (0, 0))],
        out_specs=pl.BlockSpec((tm, d), lambda i: (i, 0)),
        out_shape=jax.ShapeDtypeStruct((s, d), F32),
        compiler_params=_params("parallel"),
        name="final_norm",
    )(x, g)


def _inproj_ag_kernel(*refs, n_riders, gelu_tiles):
    (xg_ref, ss_ref, w_ref), (o_ref,), ride = _split_refs(refs, 3, 1, n_riders)
    j = pl.program_id(1)

    def project(epilogue):
        r = _row_rms_scale(ss_ref)
        for r0 in range(0, o_ref.shape[0], EPILOGUE_ROWS):
            rows = slice(r0, r0 + EPILOGUE_ROWS)
            o_ref[rows, :] = epilogue(
                r[rows] * jnp.dot(xg_ref[rows, :], w_ref[...], preferred_element_type=F32))
        ride()

    @pl.when(j < gelu_tiles)
    def _():
        project(jax.nn.gelu)

    @pl.when(j >= gelu_tiles)
    def _():
        project(jax.nn.sigmoid)


def _inproj_ag(xg, ss, w, riders=(), *, tm=1024, tn=1024):
    s, d = xg.shape
    gelu_tiles = (2 * D_HALF) // tn
    skip = GATE_COL0 // tn - gelu_tiles
    n = 2 * D_HALF + 3 * D_MODEL
    nj = n // tn
    r_in, r_out, r_shapes = _rider_specs(riders, s // tm, nj)
    outs = pl.pallas_call(
        functools.partial(_inproj_ag_kernel, n_riders=len(riders), gelu_tiles=gelu_tiles),
        grid=(s // tm, nj),
        in_specs=[pl.BlockSpec((tm, d), lambda i, j: (i, 0)),
                  pl.BlockSpec((tm, LANES), lambda i, j: (i, 0)),
                  pl.BlockSpec((d, tn),
                               lambda i, j: (0, jnp.where(j < gelu_tiles, j, j + skip)))] + r_in,
        out_specs=[pl.BlockSpec((tm, tn), lambda i, j: (i, j))] + r_out,
        out_shape=[jax.ShapeDtypeStruct((s, n), F32)] + r_shapes,
        compiler_params=_params("arbitrary", "arbitrary"),
        name="inproj_ag",
    )(xg, ss, w, *(r.stack for r in riders))
    return outs[0], outs[1:]


def _inproj_b_kernel(*refs, n_riders, chunk):
    (xg_ref, ss_ref, w_ref, wp_ref, scale_ref), (y_ref,), ride, (halo_ref,) = _split_refs(
        refs, 5, 1, n_riders, 1)
    i = pl.program_id(0)
    g = pl.program_id(1)
    tm = y_ref.shape[0]

    def pool_group(k, win):
        r = _row_rms_scale(ss_ref)
        prev = jnp.where(i > 0, halo_ref[k], 0.0)
        for r0 in range(0, tm, chunk):
            rows = slice(r0, r0 + chunk)
            x = r[rows] * jnp.dot(xg_ref[rows, :], w_ref[...], preferred_element_type=F32)
            acc = jnp.concatenate([prev, x], axis=0)
            span = 1
            while span < win:
                acc = acc[span:, :] + acc[:-span, :]
                span *= 2
            wsum = acc[POOL_HALO - (win - 1):, :]
            pos = i * tm + r0 + lax.broadcasted_iota(jnp.int32, (chunk, 1), 0) + 1
            count = jnp.minimum(pos, win).astype(F32)
            pooled = (wsum / count - x).astype(BF16)
            y = jnp.dot(pooled, wp_ref[...], preferred_element_type=F32)
            y_ref[rows, :] = (y * scale_ref[...]).astype(BF16)
            prev = x[chunk - POOL_HALO:, :]
        halo_ref[k] = prev
        ride()

    for k, win in enumerate(POOL_WINDOWS):
        pl.when(g == k)(functools.partial(pool_group, k, win))


def _inproj_b(xg, ss, w, w_pool, scale, layer, riders=(), *, tm=1024):
    s, d = xg.shape
    ng = len(POOL_WINDOWS)
    gw = D_HALF // ng
    col0 = (2 * D_HALF) // gw
    r_in, r_out, r_shapes = _rider_specs(riders, s // tm, ng)
    outs = pl.pallas_call(
        functools.partial(_inproj_b_kernel, n_riders=len(riders), chunk=POOL_MIXER_ROWS),
        grid=(s // tm, ng),
        in_specs=[pl.BlockSpec((tm, d), lambda i, g: (i, 0)),
                  pl.BlockSpec((tm, LANES), lambda i, g: (i, 0)),
                  pl.BlockSpec((d, gw), lambda i, g: (0, col0 + g)),
                  pl.BlockSpec((None, None, gw, gw), lambda i, g: (layer, g, 0, 0)),
                  pl.BlockSpec((None, 1, gw), lambda i, g: (layer, 0, g))] + r_in,
        out_specs=[pl.BlockSpec((tm, gw), lambda i, g: (i, g))] + r_out,
        out_shape=[jax.ShapeDtypeStruct((s, D_HALF), BF16)] + r_shapes,
        scratch_shapes=[pltpu.VMEM((ng, POOL_HALO, gw), F32)],
        compiler_params=_params("arbitrary", "arbitrary"),
        name="inproj_b",
    )(xg, ss, w, w_pool, scale, *(r.stack for r in riders))
    return outs[0], outs[1:]


def _inproj_c_kernel(xg_ref, ss_ref, wb_ref, wc_ref, wh_ref, cw_ref, y_ref, halo_ref,
                     *, chunk):
    i = pl.program_id(0)
    c = pl.program_id(1)
    tm = y_ref.shape[0]
    r = _row_rms_scale(ss_ref)
    prev = jnp.where(i > 0, halo_ref[c], 0.0)
    for r0 in range(0, tm, chunk):
        rows = slice(r0, r0 + chunk)
        xg = xg_ref[rows, :]
        b_gate = r[rows] * jnp.dot(xg, wb_ref[...], preferred_element_type=F32)
        c_gate = r[rows] * jnp.dot(xg, wc_ref[...], preferred_element_type=F32)
        h_c = r[rows] * jnp.dot(xg, wh_ref[...], preferred_element_type=F32)
        z = c_gate * h_c
        ext = jnp.concatenate([prev, z], axis=0)
        conv = (cw_ref[0:1, :] * ext[CONV_HALO - 2:CONV_HALO - 2 + chunk, :]
                + cw_ref[1:2, :] * ext[CONV_HALO - 1:CONV_HALO - 1 + chunk, :]
                + cw_ref[2:3, :] * z)
        y_ref[rows, :] = (b_gate * conv).astype(BF16)
        prev = z[chunk - CONV_HALO:, :]
    halo_ref[c] = prev


def _inproj_c(xg, ss, w, conv_w, layer, *, tm=2048, tc=256):
    s, d = xg.shape
    nc = D_HALF // tc
    first = (3 * D_HALF) // tc

    def w_spec(section):
        return pl.BlockSpec((d, tc), lambda i, c: (0, first + section * nc + c))

    return pl.pallas_call(
        functools.partial(_inproj_c_kernel, chunk=CONV_MIXER_ROWS),
        grid=(s // tm, nc),
        in_specs=[pl.BlockSpec((tm, d), lambda i, c: (i, 0)),
                  pl.BlockSpec((tm, LANES), lambda i, c: (i, 0)),
                  w_spec(0), w_spec(1), w_spec(2),
                  pl.BlockSpec((None, 3, tc), lambda i, c: (layer, 0, c))],
        out_specs=pl.BlockSpec((tm, tc), lambda i, c: (i, c)),
        out_shape=jax.ShapeDtypeStruct((s, D_HALF), BF16),
        scratch_shapes=[pltpu.VMEM((nc, CONV_HALO, tc), F32)],
        compiler_params=_params("arbitrary", "arbitrary"),
        name="inproj_c",
    )(xg, ss, w, w, w, conv_w)


def _branch_a_kernel(u_ref, v_ref, lng_ref, lnb_ref, ws_ref, bias_ref, y_ref):
    tm = u_ref.shape[0]
    gw = D_HALF // A_GROUPS
    v = v_ref[...]
    mu = jnp.mean(v, axis=-1, keepdims=True)
    dv = v - mu
    var = jnp.mean(dv * dv, axis=-1, keepdims=True)
    vn = ((dv * lax.rsqrt(var + EPS)) * lng_ref[...] + lnb_ref[...]).astype(BF16)
    t_idx = lax.broadcasted_iota(jnp.int32, (A_CHUNK, A_CHUNK), 0)
    s_idx = lax.broadcasted_iota(jnp.int32, (A_CHUNK, A_CHUNK), 1)
    causal = s_idx <= t_idx
    for g in range(A_GROUPS):
        wg = jnp.where(causal, ws_ref[g], 0.0).astype(BF16)
        cols = slice(g * gw, (g + 1) * gw)
        for n in range(tm // A_CHUNK):
            rows = slice(n * A_CHUNK, (n + 1) * A_CHUNK)
            mixed = jnp.dot(wg, vn[rows, cols], preferred_element_type=F32)
            mixed = mixed + bias_ref[:, cols]
            y_ref[rows, cols] = (u_ref[rows, cols] * mixed).astype(BF16)


def _branch_a(proj, ln_g, ln_b, w_s, bias, layer, *, tm=512):
    s = proj.shape[0]
    return pl.pallas_call(
        _branch_a_kernel,
        grid=(s // tm,),
        in_specs=[pl.BlockSpec((tm, D_HALF), lambda i: (i, 0)),
                  pl.BlockSpec((tm, D_HALF), lambda i: (i, 1)),
                  _layer_vec_spec(D_HALF, layer, 1),
                  _layer_vec_spec(D_HALF, layer, 1),
                  pl.BlockSpec((None, A_GROUPS, A_CHUNK, A_CHUNK),
                               lambda i: (layer, 0, 0, 0)),
                  pl.BlockSpec((A_CHUNK, D_HALF), lambda i: (0, 0))],
        out_specs=pl.BlockSpec((tm, D_HALF), lambda i: (i, 0)),
        out_shape=jax.ShapeDtypeStruct((s, D_HALF), BF16),
        compiler_params=_params("parallel"),
        name="branch_a",
    )(proj, proj, ln_g, ln_b, w_s, bias)


def _merge_kernel(ya_ref, yb_ref, yc_ref, wa_ref, wb_ref, wc_ref,
                  ga_ref, gb_ref, gc_ref, o_ref):
    m = ga_ref[...] * jnp.dot(ya_ref[...], wa_ref[...], preferred_element_type=F32)
    m = m + gb_ref[...] * jnp.dot(yb_ref[...], wb_ref[...], preferred_element_type=F32)
    m = m + gc_ref[...] * jnp.dot(yc_ref[...], wc_ref[...], preferred_element_type=F32)
    o_ref[...] = m.astype(BF16)


def _merge(ya, yb, yc, wa, wb, wc, proj, layer, *, tm=1024, tn=512):
    s = ya.shape[0]
    g0 = (2 * D_HALF) // tn
    gstep = D_MODEL // tn
    y_spec = pl.BlockSpec((tm, D_HALF), lambda i, j: (i, 0))
    w_spec = _weight_cols_spec(wa, layer, tn)

    def gate_spec(k):
        return pl.BlockSpec((tm, tn), lambda i, j: (i, g0 + k * gstep + j))

    return pl.pallas_call(
        _merge_kernel,
        grid=(s // tm, D_MODEL // tn),
        in_specs=[y_spec, y_spec, y_spec, w_spec, w_spec, w_spec,
                  gate_spec(0), gate_spec(1), gate_spec(2)],
        out_specs=pl.BlockSpec((tm, tn), lambda i, j: (i, j)),
        out_shape=jax.ShapeDtypeStruct((s, D_MODEL), BF16),
        compiler_params=_params("parallel", "arbitrary"),
        name="merge",
    )(ya, yb, yc, wa, wb, wc, proj, proj, proj)


def _matmul_residual_kernel(*refs, n_riders, emit_norm_operand):
    if emit_norm_operand:
        (a_ref, w_ref, x_ref, g_ref), (o_ref, xg_ref, ss_ref), ride = _split_refs(
            refs, 4, 3, n_riders)
    else:
        (a_ref, w_ref, x_ref), (o_ref,), ride = _split_refs(refs, 3, 1, n_riders)
    if emit_norm_operand:
        @pl.when(pl.program_id(1) == 0)
        def _():
            ss_ref[...] = jnp.zeros_like(ss_ref)

    for r0 in range(0, o_ref.shape[0], EPILOGUE_ROWS):
        rows = slice(r0, r0 + EPILOGUE_ROWS)
        y = x_ref[rows, :] + jnp.dot(a_ref[rows, :], w_ref[...], preferred_element_type=F32)
        o_ref[rows, :] = y
        if emit_norm_operand:
            xg_ref[rows, :] = (y * g_ref[...]).astype(BF16)
            ss_ref[rows, :] += _lane_partial_sumsq(y)
    ride()


def _matmul_residual(a, w, x, layer, norm_g=None, norm_layer=None, riders=(), *, tm, tn, name):
    s, k = a.shape
    n = w.shape[-1]
    nj = n // tn
    emit = norm_g is not None
    tile = pl.BlockSpec((tm, tn), lambda i, j: (i, j))
    in_specs = [pl.BlockSpec((tm, k), lambda i, j: (i, 0)), _weight_cols_spec(w, layer, tn), tile]
    out_specs = [tile]
    out_shape = [jax.ShapeDtypeStruct((s, n), F32)]
    args = [a, w, x]
    if emit:
        in_specs.append(pl.BlockSpec((None, 1, tn), lambda i, j: (norm_layer, 0, j)))
        args.append(norm_g)
        out_specs += [tile, pl.BlockSpec((tm, LANES), lambda i, j: (i, 0))]
        out_shape += [jax.ShapeDtypeStruct((s, n), BF16), jax.ShapeDtypeStruct((s, LANES), F32)]
    r_in, r_out, r_shapes = _rider_specs(riders, s // tm, nj)
    outs = pl.pallas_call(
        functools.partial(_matmul_residual_kernel, n_riders=len(riders), emit_norm_operand=emit),
        grid=(s // tm, nj),
        in_specs=in_specs + r_in,
        out_specs=out_specs + r_out,
        out_shape=out_shape + r_shapes,
        compiler_params=_params("arbitrary", "arbitrary"),
        name=name,
    )(*args, *(r.stack for r in riders))
    n_main = len(out_shape)
    return outs[:n_main], outs[n_main:]


def _ffn_up_kernel(*refs, n_riders):
    (xg_ref, ss_ref, wg_ref, wu_ref), (o_ref,), ride = _split_refs(refs, 4, 1, n_riders)
    r = _row_rms_scale(ss_ref)
    for r0 in range(0, o_ref.shape[0], EPILOGUE_ROWS):
        rows = slice(r0, r0 + EPILOGUE_ROWS)
        xg = xg_ref[rows, :]
        gate = r[rows] * jnp.dot(xg, wg_ref[...], preferred_element_type=F32)
        up = r[rows] * jnp.dot(xg, wu_ref[...], preferred_element_type=F32)
        o_ref[rows, :] = (jax.nn.silu(gate) * up).astype(BF16)
    ride()


def _ffn_up(xg, ss, wg, wu, layer, riders=(), *, tm=2048, tn=256):
    s, d = xg.shape
    f = wg.shape[-1]
    nj = f // tn
    r_in, r_out, r_shapes = _rider_specs(riders, s // tm, nj)
    outs = pl.pallas_call(
        functools.partial(_ffn_up_kernel, n_riders=len(riders)),
        grid=(s // tm, nj),
        in_specs=[pl.BlockSpec((tm, d), lambda i, j: (i, 0)),
                  pl.BlockSpec((tm, LANES), lambda i, j: (i, 0)),
                  _weight_cols_spec(wg, layer, tn),
                  _weight_cols_spec(wu, layer, tn)] + r_in,
        out_specs=[pl.BlockSpec((tm, tn), lambda i, j: (i, j))] + r_out,
        out_shape=[jax.ShapeDtypeStruct((s, f), BF16)] + r_shapes,
        compiler_params=_params("arbitrary", "arbitrary"),
        name="ffn_up",
    )(xg, ss, wg, wu, *(r.stack for r in riders))
    return outs[0], outs[1:]


def kernel(x, norm_mix_g, w_in, ln_a_g, ln_a_b, w_spatial, b_spatial, w_pool,
           pool_scale, conv_w, w_branch_a, w_branch_b, w_branch_c, w_out,
           norm_ffn_g, w_ffn_gate, w_ffn_up, w_ffn_down, final_norm_g):
    bsz, s, d = x.shape
    depth = w_in.shape[0]
    assert (bsz, d) == (1, D_MODEL) and w_in.shape[2] == N_IN
    xs = x.reshape(s, d)
    gw_a = D_HALF // A_GROUPS
    w_pool, w_branch_a, w_branch_b, w_branch_c, w_out = (
        w.astype(BF16) for w in (w_pool, w_branch_a, w_branch_b, w_branch_c, w_out))
    norm_mix_g, ln_a_g, ln_a_b, pool_scale, norm_ffn_g = (
        v[:, None, :] for v in (norm_mix_g, ln_a_g, ln_a_b, pool_scale, norm_ffn_g))

    def ffn_riders(layer):
        return (CastRider(w_ffn_gate, layer), CastRider(w_ffn_up, layer),
                CastRider(w_ffn_down, layer))

    w_in_l = w_in[0].astype(BF16)
    xg, ss = _prescale(xs, norm_mix_g, 0)
    ffn_w = None
    for l in range(depth):
        last = l + 1 == depth
        if l == 0:
            wg_rider, wu_rider, wd_rider = ffn_riders(0)
            proj, (wg, wu) = _inproj_ag(xg, ss, w_in_l, (wg_rider, wu_rider))
            yb, (wd,) = _inproj_b(xg, ss, w_in_l, w_pool, pool_scale, l, (wd_rider,))
            ffn_w = (wg, wu, wd)
        else:
            proj, _ = _inproj_ag(xg, ss, w_in_l)
            yb, _ = _inproj_b(xg, ss, w_in_l, w_pool, pool_scale, l)
        yc = _inproj_c(xg, ss, w_in_l, conv_w, l)
        bias = jnp.repeat(b_spatial[l].T, gw_a, axis=1)
        ya = _branch_a(proj, ln_a_g, ln_a_b, w_spatial, bias, l)
        merged = _merge(ya, yb, yc, w_branch_a, w_branch_b, w_branch_c, proj, l)
        (xs, xg, ss), _ = _matmul_residual(merged, w_out, xs, l, norm_ffn_g, l,
                                           tm=1024, tn=512, name="outproj")
        wg, wu, wd = ffn_w
        ff, ffn_w = _ffn_up(xg, ss, wg, wu, l, () if last else ffn_riders(l + 1))
        if last:
            (xs,), _ = _matmul_residual(ff, wd, xs, l, tm=512, tn=512, name="ffn_down")
        else:
            (xs, xg, ss), (w_in_l,) = _matmul_residual(
                ff, wd, xs, l, norm_mix_g, l + 1, (CastRider(w_in, l + 1),),
                tm=512, tn=512, name="ffn_down")
    return _final_norm(xs, final_norm_g[None, :]).reshape(bsz, s, d)
```

```python
import functools
from typing import NamedTuple

import jax
import jax.numpy as jnp
from jax import lax
from jax.experimental import pallas as pl
from jax.experimental.pallas import tpu as pltpu

D_MODEL = 4096
D_HALF = D_MODEL // 2
A_GROUPS = 8
A_CHUNK = 128
POOL_WINDOWS = (2, 4, 8, 16)
POOL_HALO = 16
CONV_HALO = 8
N_IN = 12 * D_HALF
GATE_COL0 = 6 * D_HALF
EPS = 1e-6
EPILOGUE_ROWS = 256
CONV_MIXER_ROWS = 256
POOL_MIXER_ROWS = 512
LANES = 128
BF16_SUBLANE_TILE = 16

V7X_VMEM_LIMIT_BYTES = 56 * 1024 * 1024

F32 = jnp.float32
BF16 = jnp.bfloat16


def _params(*semantics):
    return pltpu.CompilerParams(dimension_semantics=semantics,
                                vmem_limit_bytes=V7X_VMEM_LIMIT_BYTES)


def _layer_vec_spec(width, layer, grid_rank):
    if grid_rank == 1:
        return pl.BlockSpec((None, 1, width), lambda i: (layer, 0, 0))
    return pl.BlockSpec((None, 1, width), lambda i, j: (layer, 0, 0))


def _weight_cols_spec(w, layer, tn):
    if w.ndim == 2:
        return pl.BlockSpec((w.shape[0], tn), lambda i, j: (0, j))
    return pl.BlockSpec((None, w.shape[1], tn), lambda i, j: (layer, 0, j))


def _lane_partial_sumsq(x):
    sq = x * x
    acc = sq[:, :LANES]
    for c in range(LANES, x.shape[1], LANES):
        acc = acc + sq[:, c:c + LANES]
    return acc


def _row_rms_scale(ss_ref):
    ms = jnp.sum(ss_ref[...], axis=-1, keepdims=True) * (1.0 / D_MODEL)
    return lax.rsqrt(ms + EPS)


class CastRider(NamedTuple):
    stack: jax.Array
    layer: int


def _rider_block_rows(rows, n_steps):
    for rb in range(BF16_SUBLANE_TILE, rows + 1, BF16_SUBLANE_TILE):
        if rows % rb == 0 and rows // rb <= n_steps:
            return rb
    raise ValueError(f"cannot convert {rows} rows in {n_steps} grid steps")


def _rider_specs(riders, n_outer, n_inner):
    in_specs, out_specs, out_shapes = [], [], []
    for stack, layer in riders:
        _, rows, cols = stack.shape
        rb = _rider_block_rows(rows, n_outer * n_inner)
        last = rows // rb - 1

        def block(i, j, last=last):
            return jnp.minimum(i * n_inner + j, last)

        in_specs.append(pl.BlockSpec((None, rb, cols),
                                     lambda i, j, layer=layer, block=block: (layer, block(i, j), 0)))
        out_specs.append(pl.BlockSpec((rb, cols), lambda i, j, block=block: (block(i, j), 0)))
        out_shapes.append(jax.ShapeDtypeStruct((rows, cols), BF16))
    return in_specs, out_specs, out_shapes


def _split_refs(refs, n_in, n_out, n_riders, n_scratch=0):
    rider_in = refs[n_in:n_in + n_riders]
    out0 = n_in + n_riders
    rider_out = refs[out0 + n_out:out0 + n_out + n_riders]

    def ride():
        for src, dst in zip(rider_in, rider_out):
            dst[...] = src[...].astype(BF16)

    parts = (refs[:n_in], refs[out0:out0 + n_out], ride)
    if n_scratch:
        parts += (refs[len(refs) - n_scratch:],)
    return parts


def _prescale_kernel(x_ref, g_ref, xg_ref, ss_ref):
    x = x_ref[...]
    xg_ref[...] = (x * g_ref[...]).astype(BF16)
    ss_ref[...] = _lane_partial_sumsq(x)


def _prescale(x, g, layer, *, tm=512):
    s, d = x.shape
    return pl.pallas_call(
        _prescale_kernel,
        grid=(s // tm,),
        in_specs=[pl.BlockSpec((tm, d), lambda i: (i, 0)),
                  _layer_vec_spec(d, layer, 1)],
        out_specs=[pl.BlockSpec((tm, d), lambda i: (i, 0)),
                   pl.BlockSpec((tm, LANES), lambda i: (i, 0))],
        out_shape=[jax.ShapeDtypeStruct((s, d), BF16),
                   jax.ShapeDtypeStruct((s, LANES), F32)],
        compiler_params=_params("parallel"),
        name="prescale",
    )(x, g)


def _final_norm_kernel(x_ref, g_ref, o_ref):
    x = x_ref[...]
    ms = jnp.mean(x * x, axis=-1, keepdims=True)
    o_ref[...] = (x * lax.rsqrt(ms + EPS)) * g_ref[...]


def _final_norm(x, g, *, tm=512):
    s, d = x.shape
    return pl.pallas_call(
        _final_norm_kernel,
        grid=(s // tm,),
        in_specs=[pl.BlockSpec((tm, d), lambda i: (i, 0)),
                  pl.BlockSpec((1, d), lambda i: (0, 0))],
        out_specs=pl.BlockSpec((tm, d), lambda i: (i, 0)),
        out_shape=jax.ShapeDtypeStruct((s, d), F32),
        compiler_params=_params("parallel"),
        name="final_norm",
    )(x, g)


def _inproj_ag_kernel(*refs, n_riders, gelu_tiles):
    (xg_ref, ss_ref, w_ref), (o_ref,), ride = _split_refs(refs, 3, 1, n_riders)
    j = pl.program_id(1)

    def project(epilogue):
        r = _row_rms_scale(ss_ref)
        for r0 in range(0, o_ref.shape[0], EPILOGUE_ROWS):
            rows = slice(r0, r0 + EPILOGUE_ROWS)
            o_ref[rows, :] = epilogue(
                r[rows] * jnp.dot(xg_ref[rows, :], w_ref[...], preferred_element_type=F32))
        ride()

    @pl.when(j < gelu_tiles)
    def _():
        project(jax.nn.gelu)

    @pl.when(j >= gelu_tiles)
    def _():
        project(jax.nn.sigmoid)


def _inproj_ag(xg, ss, w, riders=(), *, tm=1024, tn=1024):
    s, d = xg.shape
    gelu_tiles = (2 * D_HALF) // tn
    skip = GATE_COL0 // tn - gelu_tiles
    n = 2 * D_HALF + 3 * D_MODEL
    nj = n // tn
    r_in, r_out, r_shapes = _rider_specs(riders, s // tm, nj)
    outs = pl.pallas_call(
        functools.partial(_inproj_ag_kernel, n_riders=len(riders), gelu_tiles=gelu_tiles),
        grid=(s // tm, nj),
        in_specs=[pl.BlockSpec((tm, d), lambda i, j: (i, 0)),
                  pl.BlockSpec((tm, LANES), lambda i, j: (i, 0)),
                  pl.BlockSpec((d, tn),
                               lambda i, j: (0, jnp.where(j < gelu_tiles, j, j + skip)))] + r_in,
        out_specs=[pl.BlockSpec((tm, tn), lambda i, j: (i, j))] + r_out,
        out_shape=[jax.ShapeDtypeStruct((s, n), F32)] + r_shapes,
        compiler_params=_params("arbitrary", "arbitrary"),
        name="inproj_ag",
    )(xg, ss, w, *(r.stack for r in riders))
    return outs[0], outs[1:]


def _inproj_b_kernel(*refs, n_riders, chunk):
    (xg_ref, ss_ref, w_ref, wp_ref, scale_ref), (y_ref,), ride, (halo_ref,) = _split_refs(
        refs, 5, 1, n_riders, 1)
    i = pl.program_id(0)
    g = pl.program_id(1)
    tm = y_ref.shape[0]

    def pool_group(k, win):
        r = _row_rms_scale(ss_ref)
        prev = jnp.where(i > 0, halo_ref[k], 0.0)
        for r0 in range(0, tm, chunk):
            rows = slice(r0, r0 + chunk)
            x = r[rows] * jnp.dot(xg_ref[rows, :], w_ref[...], preferred_element_type=F32)
            acc = jnp.concatenate([prev, x], axis=0)
            span = 1
            while span < win:
                acc = acc[span:, :] + acc[:-span, :]
                span *= 2
            wsum = acc[POOL_HALO - (win - 1):, :]
            pos = i * tm + r0 + lax.broadcasted_iota(jnp.int32, (chunk, 1), 0) + 1
            count = jnp.minimum(pos, win).astype(F32)
            pooled = (wsum / count - x).astype(BF16)
            y = jnp.dot(pooled, wp_ref[...], preferred_element_type=F32)
            y_ref[rows, :] = (y * scale_ref[...]).astype(BF16)
            prev = x[chunk - POOL_HALO:, :]
        halo_ref[k] = prev
        ride()

    for k, win in enumerate(POOL_WINDOWS):
        pl.when(g == k)(functools.partial(pool_group, k, win))


def _inproj_b(xg, ss, w, w_pool, scale, layer, riders=(), *, tm=1024):
    s, d = xg.shape
    ng = len(POOL_WINDOWS)
    gw = D_HALF // ng
    col0 = (2 * D_HALF) // gw
    r_in, r_out, r_shapes = _rider_specs(riders, s // tm, ng)
    outs = pl.pallas_call(
        functools.partial(_inproj_b_kernel, n_riders=len(riders), chunk=POOL_MIXER_ROWS),
        grid=(s // tm, ng),
        in_specs=[pl.BlockSpec((tm, d), lambda i, g: (i, 0)),
                  pl.BlockSpec((tm, LANES), lambda i, g: (i, 0)),
                  pl.BlockSpec((d, gw), lambda i, g: (0, col0 + g)),
                  pl.BlockSpec((None, None, gw, gw), lambda i, g: (layer, g, 0, 0)),
                  pl.BlockSpec((None, 1, gw), lambda i, g: (layer, 0, g))] + r_in,
        out_specs=[pl.BlockSpec((tm, gw), lambda i, g: (i, g))] + r_out,
        out_shape=[jax.ShapeDtypeStruct((s, D_HALF), BF16)] + r_shapes,
        scratch_shapes=[pltpu.VMEM((ng, POOL_HALO, gw), F32)],
        compiler_params=_params("arbitrary", "arbitrary"),
        name="inproj_b",
    )(xg, ss, w, w_pool, scale, *(r.stack for r in riders))
    return outs[0], outs[1:]


def _inproj_c_kernel(xg_ref, ss_ref, wb_ref, wc_ref, wh_ref, cw_ref, y_ref, halo_ref,
                     *, chunk):
    i = pl.program_id(0)
    c = pl.program_id(1)
    tm = y_ref.shape[0]
    r = _row_rms_scale(ss_ref)
    prev = jnp.where(i > 0, halo_ref[c], 0.0)
    for r0 in range(0, tm, chunk):
        rows = slice(r0, r0 + chunk)
        xg = xg_ref[rows, :]
        b_gate = r[rows] * jnp.dot(xg, wb_ref[...], preferred_element_type=F32)
        c_gate = r[rows] * jnp.dot(xg, wc_ref[...], preferred_element_type=F32)
        h_c = r[rows] * jnp.dot(xg, wh_ref[...], preferred_element_type=F32)
        z = c_gate * h_c
        ext = jnp.concatenate([prev, z], axis=0)
        conv = (cw_ref[0:1, :] * ext[CONV_HALO - 2:CONV_HALO - 2 + chunk, :]
                + cw_ref[1:2, :] * ext[CONV_HALO - 1:CONV_HALO - 1 + chunk, :]
                + cw_ref[2:3, :] * z)
        y_ref[rows, :] = (b_gate * conv).astype(BF16)
        prev = z[chunk - CONV_HALO:, :]
    halo_ref[c] = prev


def _inproj_c(xg, ss, w, conv_w, layer, *, tm=2048, tc=256):
    s, d = xg.shape
    nc = D_HALF // tc
    first = (3 * D_HALF) // tc

    def w_spec(section):
        return pl.BlockSpec((d, tc), lambda i, c: (0, first + section * nc + c))

    return pl.pallas_call(
        functools.partial(_inproj_c_kernel, chunk=CONV_MIXER_ROWS),
        grid=(s // tm, nc),
        in_specs=[pl.BlockSpec((tm, d), lambda i, c: (i, 0)),
                  pl.BlockSpec((tm, LANES), lambda i, c: (i, 0)),
                  w_spec(0), w_spec(1), w_spec(2),
                  pl.BlockSpec((None, 3, tc), lambda i, c: (layer, 0, c))],
        out_specs=pl.BlockSpec((tm, tc), lambda i, c: (i, c)),
        out_shape=jax.ShapeDtypeStruct((s, D_HALF), BF16),
        scratch_shapes=[pltpu.VMEM((nc, CONV_HALO, tc), F32)],
        compiler_params=_params("arbitrary", "arbitrary"),
        name="inproj_c",
    )(xg, ss, w, w, w, conv_w)


def _branch_a_kernel(u_ref, v_ref, lng_ref, lnb_ref, ws_ref, bias_ref, y_ref):
    tm = u_ref.shape[0]
    gw = D_HALF // A_GROUPS
    v = v_ref[...]
    mu = jnp.mean(v, axis=-1, keepdims=True)
    dv = v - mu
    var = jnp.mean(dv * dv, axis=-1, keepdims=True)
    vn = ((dv * lax.rsqrt(var + EPS)) * lng_ref[...] + lnb_ref[...]).astype(BF16)
    t_idx = lax.broadcasted_iota(jnp.int32, (A_CHUNK, A_CHUNK), 0)
    s_idx = lax.broadcasted_iota(jnp.int32, (A_CHUNK, A_CHUNK), 1)
    causal = s_idx <= t_idx
    for g in range(A_GROUPS):
        wg = jnp.where(causal, ws_ref[g], 0.0).astype(BF16)
        cols = slice(g * gw, (g + 1) * gw)
        for n in range(tm // A_CHUNK):
            rows = slice(n * A_CHUNK, (n + 1) * A_CHUNK)
            mixed = jnp.dot(wg, vn[rows, cols], preferred_element_type=F32)
            mixed = mixed + bias_ref[:, cols]
            y_ref[rows, cols] = (u_ref[rows, cols] * mixed).astype(BF16)


def _branch_a(proj, ln_g, ln_b, w_s, bias, layer, *, tm=512):
    s = proj.shape[0]
    return pl.pallas_call(
        _branch_a_kernel,
        grid=(s // tm,),
        in_specs=[pl.BlockSpec((tm, D_HALF), lambda i: (i, 0)),
                  pl.BlockSpec((tm, D_HALF), lambda i: (i, 1)),
                  _layer_vec_spec(D_HALF, layer, 1),
                  _layer_vec_spec(D_HALF, layer, 1),
                  pl.BlockSpec((None, A_GROUPS, A_CHUNK, A_CHUNK),
                               lambda i: (layer, 0, 0, 0)),
                  pl.BlockSpec((A_CHUNK, D_HALF), lambda i: (0, 0))],
        out_specs=pl.BlockSpec((tm, D_HALF), lambda i: (i, 0)),
        out_shape=jax.ShapeDtypeStruct((s, D_HALF), BF16),
        compiler_params=_params("parallel"),
        name="branch_a",
    )(proj, proj, ln_g, ln_b, w_s, bias)


def _merge_kernel(ya_ref, yb_ref, yc_ref, wa_ref, wb_ref, wc_ref,
                  ga_ref, gb_ref, gc_ref, o_ref):
    m = ga_ref[...] * jnp.dot(ya_ref[...], wa_ref[...], preferred_element_type=F32)
    m = m + gb_ref[...] * jnp.dot(yb_ref[...], wb_ref[...], preferred_element_type=F32)
    m = m + gc_ref[...] * jnp.dot(yc_ref[...], wc_ref[...], preferred_element_type=F32)
    o_ref[...] = m.astype(BF16)


def _merge(ya, yb, yc, wa, wb, wc, proj, layer, *, tm=1024, tn=512):
    s = ya.shape[0]
    g0 = (2 * D_HALF) // tn
    gstep = D_MODEL // tn
    y_spec = pl.BlockSpec((tm, D_HALF), lambda i, j: (i, 0))
    w_spec = _weight_cols_spec(wa, layer, tn)

    def gate_spec(k):
        return pl.BlockSpec((tm, tn), lambda i, j: (i, g0 + k * gstep + j))

    return pl.pallas_call(
        _merge_kernel,
        grid=(s // tm, D_MODEL // tn),
        in_specs=[y_spec, y_spec, y_spec, w_spec, w_spec, w_spec,
                  gate_spec(0), gate_spec(1), gate_spec(2)],
        out_specs=pl.BlockSpec((tm, tn), lambda i, j: (i, j)),
        out_shape=jax.ShapeDtypeStruct((s, D_MODEL), BF16),
        compiler_params=_params("parallel", "arbitrary"),
        name="merge",
    )(ya, yb, yc, wa, wb, wc, proj, proj, proj)


def _matmul_residual_kernel(*refs, n_riders, emit_norm_operand):
    if emit_norm_operand:
        (a_ref, w_ref, x_ref, g_ref), (o_ref, xg_ref, ss_ref), ride = _split_refs(
            refs, 4, 3, n_riders)
    else:
        (a_ref, w_ref, x_ref), (o_ref,), ride = _split_refs(refs, 3, 1, n_riders)
    if emit_norm_operand:
        @pl.when(pl.program_id(1) == 0)
        def _():
            ss_ref[...] = jnp.zeros_like(ss_ref)

    for r0 in range(0, o_ref.shape[0], EPILOGUE_ROWS):
        rows = slice(r0, r0 + EPILOGUE_ROWS)
        y = x_ref[rows, :] + jnp.dot(a_ref[rows, :], w_ref[...], preferred_element_type=F32)
        o_ref[rows, :] = y
        if emit_norm_operand:
            xg_ref[rows, :] = (y * g_ref[...]).astype(BF16)
            ss_ref[rows, :] += _lane_partial_sumsq(y)
    ride()


def _matmul_residual(a, w, x, layer, norm_g=None, norm_layer=None, riders=(), *, tm, tn, name,
                     single_buffer_a=False):
    s, k = a.shape
    n = w.shape[-1]
    nj = n // tn
    emit = norm_g is not None
    tile = pl.BlockSpec((tm, tn), lambda i, j: (i, j))
    a_kw = dict(pipeline_mode=pl.Buffered(1)) if single_buffer_a else {}
    in_specs = [pl.BlockSpec((tm, k), lambda i, j: (i, 0), **a_kw),
                _weight_cols_spec(w, layer, tn), tile]
    out_specs = [tile]
    out_shape = [jax.ShapeDtypeStruct((s, n), F32)]
    args = [a, w, x]
    if emit:
        in_specs.append(pl.BlockSpec((None, 1, tn), lambda i, j: (norm_layer, 0, j)))
        args.append(norm_g)
        out_specs += [tile, pl.BlockSpec((tm, LANES), lambda i, j: (i, 0))]
        out_shape += [jax.ShapeDtypeStruct((s, n), BF16), jax.ShapeDtypeStruct((s, LANES), F32)]
    r_in, r_out, r_shapes = _rider_specs(riders, s // tm, nj)
    outs = pl.pallas_call(
        functools.partial(_matmul_residual_kernel, n_riders=len(riders), emit_norm_operand=emit),
        grid=(s // tm, nj),
        in_specs=in_specs + r_in,
        out_specs=out_specs + r_out,
        out_shape=out_shape + r_shapes,
        compiler_params=_params("arbitrary", "arbitrary"),
        name=name,
    )(*args, *(r.stack for r in riders))
    n_main = len(out_shape)
    return outs[:n_main], outs[n_main:]


def _ffn_up_kernel(*refs, n_riders):
    (xg_ref, ss_ref, wg_ref, wu_ref), (o_ref,), ride = _split_refs(refs, 4, 1, n_riders)
    r = _row_rms_scale(ss_ref)
    for r0 in range(0, o_ref.shape[0], EPILOGUE_ROWS):
        rows = slice(r0, r0 + EPILOGUE_ROWS)
        xg = xg_ref[rows, :]
        gate = r[rows] * jnp.dot(xg, wg_ref[...], preferred_element_type=F32)
        up = r[rows] * jnp.dot(xg, wu_ref[...], preferred_element_type=F32)
        o_ref[rows, :] = (jax.nn.silu(gate) * up).astype(BF16)
    ride()


def _ffn_up(xg, ss, wg, wu, layer, riders=(), *, tm=2048, tn=256):
    s, d = xg.shape
    f = wg.shape[-1]
    nj = f // tn
    r_in, r_out, r_shapes = _rider_specs(riders, s // tm, nj)
    outs = pl.pallas_call(
        functools.partial(_ffn_up_kernel, n_riders=len(riders)),
        grid=(s // tm, nj),
        in_specs=[pl.BlockSpec((tm, d), lambda i, j: (i, 0)),
                  pl.BlockSpec((tm, LANES), lambda i, j: (i, 0)),
                  _weight_cols_spec(wg, layer, tn),
                  _weight_cols_spec(wu, layer, tn)] + r_in,
        out_specs=[pl.BlockSpec((tm, tn), lambda i, j: (i, j))] + r_out,
        out_shape=[jax.ShapeDtypeStruct((s, f), BF16)] + r_shapes,
        compiler_params=_params("arbitrary", "arbitrary"),
        name="ffn_up",
    )(xg, ss, wg, wu, *(r.stack for r in riders))
    return outs[0], outs[1:]


def kernel(x, norm_mix_g, w_in, ln_a_g, ln_a_b, w_spatial, b_spatial, w_pool,
           pool_scale, conv_w, w_branch_a, w_branch_b, w_branch_c, w_out,
           norm_ffn_g, w_ffn_gate, w_ffn_up, w_ffn_down, final_norm_g):
    bsz, s, d = x.shape
    depth = w_in.shape[0]
    assert (bsz, d) == (1, D_MODEL) and w_in.shape[2] == N_IN
    xs = x.reshape(s, d)
    gw_a = D_HALF // A_GROUPS
    w_pool, w_branch_a, w_branch_b, w_branch_c, w_out = (
        w.astype(BF16) for w in (w_pool, w_branch_a, w_branch_b, w_branch_c, w_out))
    norm_mix_g, ln_a_g, ln_a_b, pool_scale, norm_ffn_g = (
        v[:, None, :] for v in (norm_mix_g, ln_a_g, ln_a_b, pool_scale, norm_ffn_g))

    def ffn_riders(layer):
        return (CastRider(w_ffn_gate, layer), CastRider(w_ffn_up, layer),
                CastRider(w_ffn_down, layer))

    w_in_l = w_in[0].astype(BF16)
    xg, ss = _prescale(xs, norm_mix_g, 0)
    ffn_w = None
    for l in range(depth):
        last = l + 1 == depth
        if l == 0:
            wg_rider, wu_rider, wd_rider = ffn_riders(0)
            proj, (wg, wu) = _inproj_ag(xg, ss, w_in_l, (wg_rider, wu_rider))
            yb, (wd,) = _inproj_b(xg, ss, w_in_l, w_pool, pool_scale, l, (wd_rider,))
            ffn_w = (wg, wu, wd)
        else:
            proj, _ = _inproj_ag(xg, ss, w_in_l)
            yb, _ = _inproj_b(xg, ss, w_in_l, w_pool, pool_scale, l)
        yc = _inproj_c(xg, ss, w_in_l, conv_w, l)
        bias = jnp.repeat(b_spatial[l].T, gw_a, axis=1)
        ya = _branch_a(proj, ln_a_g, ln_a_b, w_spatial, bias, l)
        merged = _merge(ya, yb, yc, w_branch_a, w_branch_b, w_branch_c, proj, l)
        (xs, xg, ss), _ = _matmul_residual(merged, w_out, xs, l, norm_ffn_g, l,
                                           tm=1024, tn=1024, name="outproj",
                                           single_buffer_a=True)
        wg, wu, wd = ffn_w
        ff, ffn_w = _ffn_up(xg, ss, wg, wu, l, () if last else ffn_riders(l + 1))
        if last:
            (xs,), _ = _matmul_residual(ff, wd, xs, l, tm=512, tn=512, name="ffn_down")
        else:
            (xs, xg, ss), (w_in_l,) = _matmul_residual(
                ff, wd, xs, l, norm_mix_g, l + 1, (CastRider(w_in, l + 1),),
                tm=512, tn=512, name="ffn_down")
    return _final_norm(xs, final_norm_g[None, :]).reshape(bsz, s, d)
```
